```python
import math
import jax, jax.numpy as jnp
from jax import lax
import numpy as np

D_MODEL = 1024
BATCH = 16
SEQ = 2048
DEPTH = 2
DEC_BATCH = 8
DEC_SEQ = 4096
PAST_LEN = 128

HEAD_DIM = 64
ATT_HEADS = 6
ATT_W = ATT_HEADS * HEAD_DIM
DILATION_PATTERNS = ((128, 1), (512, 4), (2048, 16))
ATT_BLOCK = 64
RET_HEADS = 4
RET_HEAD_DIM = 96
RET_W = RET_HEADS * RET_HEAD_DIM
RET_CHUNK = 128
CONV_W = D_MODEL - ATT_W - RET_W
CONV_WIDTH = 3
MIX_W = ATT_W + RET_W + CONV_W
MIX_IN_W = 3 * ATT_W + 4 * RET_W + 3 * CONV_W
D_FF = 2816
NORM_EPS = 1e-6
NEG_INF = -1e30

kernel_name = "hybrid_bidir_encoder_dilated_retention_shortconv"


def rmsnorm(x, g):
    xf = x.astype(jnp.float32)
    y = xf * lax.rsqrt(jnp.mean(xf * xf, axis=-1, keepdims=True) + NORM_EPS) * g.astype(jnp.float32)
    return y.astype(x.dtype)


def swiglu(x, w_gate, w_up, w_down):
    return (jax.nn.silu(x @ w_gate) * (x @ w_up)) @ w_down


def _alibi_slope_list(n):
    def pow2(m):
        start = 2.0 ** (-8.0 / m)
        return [start ** (i + 1) for i in range(m)]
    if math.log2(n).is_integer():
        return pow2(n)
    c = 2 ** math.floor(math.log2(n))
    return pow2(c) + _alibi_slope_list(2 * c)[0::2][: n - c]


def alibi_slopes(n):
    return jnp.array(_alibi_slope_list(n), dtype=jnp.float32)


def dilated_window_attention(q, k, v, slopes, window, dilation):
    B, S, H, dh = q.shape
    half = window // (2 * dilation)
    L = S // dilation
    nb = -(-L // ATT_BLOCK)
    Lp = nb * ATT_BLOCK

    def to_sub(t):
        t = t.reshape(B, L, dilation, H, dh).transpose(0, 2, 3, 1, 4)
        return jnp.pad(t, ((0, 0), (0, 0), (0, 0), (0, Lp - L), (0, 0)))

    def windows(t):
        tp = jnp.pad(t, ((0, 0), (0, 0), (0, 0), (ATT_BLOCK, ATT_BLOCK), (0, 0)))
        tb = tp.reshape(B, dilation, H, nb + 2, ATT_BLOCK, dh)
        return jnp.concatenate([tb[:, :, :, :-2], tb[:, :, :, 1:-1], tb[:, :, :, 2:]], axis=4)

    qb = to_sub(q).reshape(B, dilation, H, nb, ATT_BLOCK, dh)
    kw = windows(to_sub(k))
    vw = windows(to_sub(v))
    s = jnp.einsum("brhnqd,brhnkd->brhnqk", qb, kw, preferred_element_type=jnp.float32) * (dh ** -0.5)

    qi = jnp.arange(nb)[:, None, None] * ATT_BLOCK + jnp.arange(ATT_BLOCK)[None, :, None]
    kj = (jnp.arange(nb)[:, None, None] - 1) * ATT_BLOCK + jnp.arange(3 * ATT_BLOCK)[None, None, :]
    rel = jnp.abs(qi - kj)
    valid = (rel <= half) & (kj >= 0) & (kj < L)
    bias = -slopes[:, None, None, None] * (dilation * rel).astype(jnp.float32)[None]
    s = jnp.where(valid, s + bias, NEG_INF)

    m = jnp.max(s, axis=-1, keepdims=True)
    p = jnp.exp(s - m)
    den = jnp.sum(p, axis=-1, keepdims=True)
    o = jnp.einsum("brhnqk,brhnkd->brhnqd", p, vw.astype(jnp.float32)) / den
    lse = (m + jnp.log(den))[..., 0]

    o = o.reshape(B, dilation, H, Lp, dh)[:, :, :, :L].transpose(0, 3, 1, 2, 4).reshape(B, S, H, dh)
    lse = lse.reshape(B, dilation, H, Lp)[:, :, :, :L].transpose(0, 3, 1, 2).reshape(B, S, H)
    return o, lse


def retention_one_direction(q, k, v, log_gamma, include_diag):
    B, H, S, dk = q.shape
    dv = v.shape[-1]
    C = RET_CHUNK
    n = S // C
    qc = q.reshape(B, H, n, C, dk)
    kc = k.reshape(B, H, n, C, dk)
    vc = v.reshape(B, H, n, C, dv)
    pos = jnp.arange(C, dtype=jnp.float32)
    rel = pos[:, None] - pos[None, :]
    mask = (rel >= 0) if include_diag else (rel > 0)
    decay = jnp.where(mask, jnp.exp(log_gamma[:, None, None] * jnp.maximum(rel, 0.0)), 0.0)
    scores = jnp.einsum("bhnqd,bhnkd->bhnqk", qc, kc) * decay[None, :, None]
    o_intra = jnp.einsum("bhnqk,bhnke->bhnqe", scores, vc)

    k_w = jnp.exp(log_gamma[:, None] * (C - 1 - pos))
    kv = jnp.einsum("bhnkd,bhnke->nbhde", kc * k_w[None, :, None, :, None], vc)
    g_chunk = jnp.exp(log_gamma * C)[None, :, None, None]

    def step(state, kv_i):
        return state * g_chunk + kv_i, state

    _, states = lax.scan(step, jnp.zeros((B, H, dk, dv), jnp.float32), kv)
    q_w = jnp.exp(log_gamma[:, None] * (pos + 1.0))
    o_cross = jnp.einsum("bhnqd,nbhde->bhnqe", qc * q_w[None, :, None, :, None], states)
    return (o_intra + o_cross).reshape(B, H, S, dv)


def hybrid_mixer(h, w_in, w_out, conv_w, ret_decay_logit):
    B, S, _ = h.shape
    z = h @ w_in
    sizes = [ATT_W] * 3 + [RET_W] * 4 + [CONV_W] * 3
    q_a, k_a, v_a, q_r, k_r, v_r, g_r, b_c, c_c, u_c = jnp.split(z, np.cumsum(sizes)[:-1].tolist(), axis=-1)

    qh = q_a.reshape(B, S, ATT_HEADS, HEAD_DIM)
    kh = k_a.reshape(B, S, ATT_HEADS, HEAD_DIM)
    vh = v_a.reshape(B, S, ATT_HEADS, HEAD_DIM)
    slopes = alibi_slopes(ATT_HEADS)
    outs, lses = [], []
    for window, dilation in DILATION_PATTERNS:
        o, l = dilated_window_attention(qh, kh, vh, slopes, window, dilation)
        outs.append(o)
        lses.append(l)
    wts = jax.nn.softmax(jnp.stack(lses, axis=0), axis=0)
    att = jnp.sum(wts[..., None] * jnp.stack(outs, axis=0), axis=0).reshape(B, S, ATT_W).astype(h.dtype)

    def rh(t):
        return t.reshape(B, S, RET_HEADS, RET_HEAD_DIM).transpose(0, 2, 1, 3).astype(jnp.float32)
    qr, kr, vr = rh(q_r), rh(k_r) * (RET_HEAD_DIM ** -0.5), rh(v_r)
    log_gamma = jax.nn.log_sigmoid(ret_decay_logit.astype(jnp.float32))
    o_fwd = retention_one_direction(qr, kr, vr, log_gamma[0], True)
    o_bwd = jnp.flip(retention_one_direction(jnp.flip(qr, 2), jnp.flip(kr, 2), jnp.flip(vr, 2), log_gamma[1], False), 2)
    o = o_fwd + o_bwd
    mu = jnp.mean(o, axis=-1, keepdims=True)
    var = jnp.mean(jnp.square(o - mu), axis=-1, keepdims=True)
    o = (o - mu) * lax.rsqrt(var + NORM_EPS)
    ret = o.transpose(0, 2, 1, 3).reshape(B, S, RET_W).astype(h.dtype) * jax.nn.silu(g_r)

    u = c_c * u_c
    conv = lax.conv_general_dilated(u, conv_w[:, None, :], window_strides=(1,),
                                    padding=[(CONV_WIDTH // 2, CONV_WIDTH // 2)],
                                    dimension_numbers=("NWC", "WIO", "NWC"),
                                    feature_group_count=CONV_W)
    cv = b_c * conv

    return jnp.concatenate([att, ret, cv], axis=-1) @ w_out


def trunk(x, norm_gain, ffn1_w_gate, ffn1_w_up, ffn1_w_down, w_mix_in, conv_w,
          ret_decay_logit, w_mix_out, ffn2_w_gate, ffn2_w_up, ffn2_w_down):
    for l in range(DEPTH):
        g = norm_gain[l]
        h = swiglu(rmsnorm(x, g[0]), ffn1_w_gate[l], ffn1_w_up[l], ffn1_w_down[l])
        x = x + 0.5 * rmsnorm(h, g[1])
        h = hybrid_mixer(rmsnorm(x, g[2]), w_mix_in[l], w_mix_out[l], conv_w[l], ret_decay_logit[l])
        x = x + rmsnorm(h, g[3])
        h = swiglu(rmsnorm(x, g[4]), ffn2_w_gate[l], ffn2_w_up[l], ffn2_w_down[l])
        x = x + 0.5 * rmsnorm(h, g[5])
    return x


def setup_inputs(seed: int = 0) -> dict:
    key = jax.random.key(seed)
    ks = jax.random.split(key, 16)

    def nrm(k, shape, scale):
        return jax.random.normal(k, shape, jnp.float32) * scale

    base_logit = jnp.log(jnp.exp2(5.0 + jnp.arange(RET_HEADS, dtype=jnp.float32)) - 1.0)
    return {
        "x_prompt": nrm(ks[0], (BATCH, SEQ, D_MODEL), 1.0),
        "x_sample": nrm(ks[1], (DEC_BATCH, DEC_SEQ, D_MODEL), 1.0),
        "norm_gain": 1.0 + nrm(ks[2], (DEPTH, 6, D_MODEL), 0.02),
        "ffn1_w_gate": nrm(ks[3], (DEPTH, D_MODEL, D_FF), D_MODEL ** -0.5),
        "ffn1_w_up": nrm(ks[4], (DEPTH, D_MODEL, D_FF), D_MODEL ** -0.5),
        "ffn1_w_down": nrm(ks[5], (DEPTH, D_FF, D_MODEL), D_FF ** -0.5),
        "w_mix_in": nrm(ks[6], (DEPTH, D_MODEL, MIX_IN_W), D_MODEL ** -0.5),
        "conv_w": nrm(ks[7], (DEPTH, CONV_WIDTH, CONV_W), CONV_WIDTH ** -0.5),
        "ret_decay_logit": base_logit[None, None, :] + nrm(ks[8], (DEPTH, 2, RET_HEADS), 0.1),
        "w_mix_out": nrm(ks[9], (DEPTH, MIX_W, D_MODEL), MIX_W ** -0.5),
        "ffn2_w_gate": nrm(ks[10], (DEPTH, D_MODEL, D_FF), D_MODEL ** -0.5),
        "ffn2_w_up": nrm(ks[11], (DEPTH, D_MODEL, D_FF), D_MODEL ** -0.5),
        "ffn2_w_down": nrm(ks[12], (DEPTH, D_FF, D_MODEL), D_FF ** -0.5),
    }


def reference(x_prompt, x_sample, norm_gain, ffn1_w_gate, ffn1_w_up, ffn1_w_down, w_mix_in, conv_w,
              ret_decay_logit, w_mix_out, ffn2_w_gate, ffn2_w_up, ffn2_w_down):
    y_prompt = trunk(x_prompt, norm_gain, ffn1_w_gate, ffn1_w_up, ffn1_w_down, w_mix_in, conv_w,
                     ret_decay_logit, w_mix_out, ffn2_w_gate, ffn2_w_up, ffn2_w_down)
    y_sample = trunk(x_sample, norm_gain, ffn1_w_gate, ffn1_w_up, ffn1_w_down, w_mix_in, conv_w,
                     ret_decay_logit, w_mix_out, ffn2_w_gate, ffn2_w_up, ffn2_w_down)
    return (y_prompt, y_sample)
```

```python
import functools
import math

import jax
import jax.numpy as jnp
import numpy as np
from jax import lax
from jax.experimental import pallas as pl
from jax.experimental.pallas import tpu as pltpu

D_MODEL = 1024
D_FF = 2816
DEPTH = 2
HEAD_DIM = 64
ATT_HEADS = 6
ATT_W = ATT_HEADS * HEAD_DIM
DILATIONS = (1, 4, 16)
ATT_HALF = 64
RET_HEADS = 4
RET_HEAD_DIM = 96
RET_W = RET_HEADS * RET_HEAD_DIM
RET_CHUNK = 128
CONV_W = 256
NORM_EPS = 1e-6
NEG_INF = -1e30

LANES = 128
RET_PAD_W = RET_HEADS * LANES
ATT_TQ = 128
ATT_TK = ATT_TQ + 2 * ATT_HALF
PERM_T = 256
FF_CHUNK = 256
TOKEN_TILE = 512
VMEM_LIMIT = 56 * 1024 * 1024

_BF = jnp.bfloat16
_F32 = jnp.float32
_NT = (((1,), (1,)), ((), ()))
_TN = (((0,), (0,)), ((), ()))


def _alibi_slope_list(n):
    def pow2(m):
        start = 2.0 ** (-8.0 / m)
        return [start ** (i + 1) for i in range(m)]
    if math.log2(n).is_integer():
        return pow2(n)
    c = 2 ** math.floor(math.log2(n))
    return pow2(c) + _alibi_slope_list(2 * c)[0::2][: n - c]


_SLOPES = [float(np.float32(s)) for s in _alibi_slope_list(ATT_HEADS)]


def _rms(x, g):
    return x * lax.rsqrt(jnp.mean(x * x, axis=-1, keepdims=True) + NORM_EPS) * g


def _params(n_axes):
    return pltpu.CompilerParams(dimension_semantics=("arbitrary",) * n_axes,
                                vmem_limit_bytes=VMEM_LIMIT)


def _resident(shape):
    return pl.BlockSpec(shape, lambda *_: (0,) * len(shape), pipeline_mode=pl.Buffered(1))


def _ffn_kernel(x_ref, gain_ref, wg_ref, wu_ref, wd_ref, o_ref):
    x = x_ref[...]
    h = _rms(x, gain_ref[0:1, :]).astype(_BF)
    acc = jnp.zeros(x.shape, _F32)
    for c in range(D_FF // FF_CHUNK):
        sl = slice(c * FF_CHUNK, (c + 1) * FF_CHUNK)
        g = jnp.dot(h, wg_ref[:, sl], preferred_element_type=_F32)
        u = jnp.dot(h, wu_ref[:, sl], preferred_element_type=_F32)
        a = (g / (1.0 + jnp.exp(-g)) * u).astype(_BF)
        acc = acc + jnp.dot(a, wd_ref[sl, :], preferred_element_type=_F32)
    o_ref[...] = x + 0.5 * _rms(acc, gain_ref[1:2, :])


def _ffn(x, gains, wg, wu, wd):
    n = x.shape[0]
    tm = TOKEN_TILE
    row = lambda i: (i, 0)
    return pl.pallas_call(
        _ffn_kernel,
        out_shape=jax.ShapeDtypeStruct(x.shape, x.dtype),
        grid=(n // tm,),
        in_specs=[pl.BlockSpec((tm, D_MODEL), row), _resident((2, D_MODEL)),
                  _resident((D_MODEL, D_FF)), _resident((D_MODEL, D_FF)), _resident((D_FF, D_MODEL))],
        out_specs=pl.BlockSpec((tm, D_MODEL), row),
        compiler_params=_params(1),
        name="ffn",
    )(x, gains, wg, wu, wd)


_IN_ATT = 3 * ATT_W
_IN_RET = RET_HEADS * 3 * LANES
_IN_GATE = RET_PAD_W
_IN_CONV = 3 * CONV_W
_IN_TOTAL = _IN_ATT + _IN_RET + _IN_GATE + _IN_CONV


def _inproj_kernel(x_ref, gain_ref, w_ref, att_ref, ret_ref, gate_ref, b_ref, u_ref):
    h = _rms(x_ref[...], gain_ref[...]).astype(_BF)
    c0 = 0
    za = jnp.dot(h, w_ref[:, c0:c0 + _IN_ATT], preferred_element_type=_F32)
    att_ref[:, :ATT_W] = (za[:, :ATT_W] * (HEAD_DIM ** -0.5)).astype(_BF)
    att_ref[:, ATT_W:] = za[:, ATT_W:].astype(_BF)
    c0 += _IN_ATT
    ret_ref[...] = jnp.dot(h, w_ref[:, c0:c0 + _IN_RET], preferred_element_type=_F32).astype(_BF)
    c0 += _IN_RET
    gate_ref[...] = jnp.dot(h, w_ref[:, c0:c0 + _IN_GATE], preferred_element_type=_F32)
    c0 += _IN_GATE
    zc = jnp.dot(h, w_ref[:, c0:c0 + _IN_CONV], preferred_element_type=_F32)
    b_ref[...] = zc[:, :CONV_W]
    u_ref[...] = zc[:, CONV_W:2 * CONV_W] * zc[:, 2 * CONV_W:]


def _inproj(x, gain, w):
    n = x.shape[0]
    tm = TOKEN_TILE
    row = lambda i: (i, 0)
    widths = (_IN_ATT, _IN_RET, _IN_GATE, CONV_W, CONV_W)
    dtypes = (_BF, _BF, _F32, _F32, _F32)
    return pl.pallas_call(
        _inproj_kernel,
        out_shape=[jax.ShapeDtypeStruct((n, w_), d_) for w_, d_ in zip(widths, dtypes)],
        grid=(n // tm,),
        in_specs=[pl.BlockSpec((tm, D_MODEL), row), _resident((1, D_MODEL)),
                  _resident((D_MODEL, _IN_TOTAL))],
        out_specs=[pl.BlockSpec((tm, w_), row) for w_ in widths],
        compiler_params=_params(1),
        name="inproj",
    )(x, gain, w)


def _perm_matrices(d):
    c = PERM_T // d
    sh = int(math.log2(c))
    row = lax.broadcasted_iota(jnp.int32, (PERM_T, PERM_T), 0)
    col = lax.broadcasted_iota(jnp.int32, (PERM_T, PERM_T), 1)
    src = ((row & (c - 1)) * d) + (row >> sh)
    dst = ((col & (c - 1)) * d) + (col >> sh)
    return (jnp.where(col == src, 1.0, 0.0).astype(_BF), jnp.where(row == dst, 1.0, 0.0).astype(_BF))


def _att_kernel(q_ref, k_ref, v_ref, o_ref, qd, ka, kb, va, vb, od, ld, bias_ref, *, seq):
    pair = pl.program_id(1)
    lane = lax.broadcasted_iota(jnp.int32, (1, LANES), 1)
    head_a = lane < HEAD_DIM
    zero_bf = jnp.zeros((), _BF)
    n_tiles = seq // PERM_T

    def slope(j):
        s = jnp.float32(_SLOPES[j])
        for p in range(1, ATT_HEADS // 2):
            s = jnp.where(pair == p, jnp.float32(_SLOPES[2 * p + j]), s)
        return s

    for w, d in enumerate(DILATIONS):
        sub_len = seq // d
        pad_len = sub_len + 2 * ATT_HALF
        chunk = PERM_T // d
        n_blocks = sub_len // ATT_TQ

        qi = lax.broadcasted_iota(jnp.int32, (ATT_TQ, ATT_TK), 0)
        kc = lax.broadcasted_iota(jnp.int32, (ATT_TQ, ATT_TK), 1)
        rel = jnp.abs(qi + ATT_HALF - kc)
        dist = (d * rel).astype(_F32)
        for j in range(2):
            sl = slope(j)
            for variant in range(4):
                valid = rel <= ATT_HALF
                if variant & 1:
                    valid = valid & (kc >= ATT_HALF)
                if variant & 2:
                    valid = valid & (kc < ATT_TK - ATT_HALF)
                bias_ref[j, variant] = jnp.where(valid, -sl * dist, NEG_INF)

        zeros = jnp.zeros(ka.shape, _BF)
        ka[...] = zeros
        kb[...] = zeros
        va[...] = zeros
        vb[...] = zeros
        if d > 1:
            perm, perm_t = _perm_matrices(d)

        def regroup(i, carry):
            t0 = pl.multiple_of(i * PERM_T, PERM_T)
            tiles = [r[0, pl.ds(t0, PERM_T), :] for r in (q_ref, k_ref, v_ref)]
            if d > 1:
                tiles = [jnp.dot(perm, t, preferred_element_type=_F32).astype(_BF) for t in tiles]
            qt, kt, vt = tiles
            kta, ktb = jnp.where(head_a, kt, zero_bf), jnp.where(head_a, zero_bf, kt)
            vta, vtb = jnp.where(head_a, vt, zero_bf), jnp.where(head_a, zero_bf, vt)
            for r in range(d):
                rows = slice(r * chunk, (r + 1) * chunk)
                dq = pl.multiple_of(r * sub_len + i * chunk, chunk)
                dk = pl.multiple_of(r * pad_len + ATT_HALF + i * chunk, min(chunk, ATT_HALF))
                qd[pl.ds(dq, chunk), :] = qt[rows]
                ka[pl.ds(dk, chunk), :] = kta[rows]
                kb[pl.ds(dk, chunk), :] = ktb[rows]
                va[pl.ds(dk, chunk), :] = vta[rows]
                vb[pl.ds(dk, chunk), :] = vtb[rows]
            return carry

        lax.fori_loop(0, n_tiles, regroup, 0)

        def block(blk, carry):
            r = blk // n_blocks
            b = blk - r * n_blocks
            q0 = pl.multiple_of(blk * ATT_TQ, ATT_TQ)
            k0 = pl.multiple_of(blk * ATT_TQ + r * (2 * ATT_HALF), ATT_TQ)
            variant = (b == 0).astype(jnp.int32) + 2 * (b == n_blocks - 1).astype(jnp.int32)
            q = qd[pl.ds(q0, ATT_TQ), :]
            acc = jnp.zeros((ATT_TQ, LANES), _F32)
            stats = []
            for j, (kr, vr) in enumerate(((ka, va), (kb, vb))):
                s = lax.dot_general(q, kr[pl.ds(k0, ATT_TK), :], _NT, preferred_element_type=_F32)
                s = s + bias_ref[j, variant]
                m = jnp.max(s, axis=-1, keepdims=True)
                p = jnp.exp(s - m)
                den = jnp.sum(p, axis=-1, keepdims=True)
                acc = acc + jnp.dot(p.astype(_BF), vr[pl.ds(k0, ATT_TK), :], preferred_element_type=_F32)
                stats.append((m, den))
            (m_a, den_a), (m_b, den_b) = stats
            od[w, pl.ds(q0, ATT_TQ), :] = acc / jnp.where(head_a, den_a, den_b)
            ld[w, pl.ds(q0, ATT_TQ), :] = jnp.where(head_a, m_a + jnp.log(den_a), m_b + jnp.log(den_b))
            return carry

        lax.fori_loop(0, d * n_blocks, block, 0)

    perms_t = {d: _perm_matrices(d)[1] for d in DILATIONS[1:]}

    def merge(i, carry):
        t0 = pl.multiple_of(i * PERM_T, PERM_T)
        outs = [od[0, pl.ds(t0, PERM_T), :]]
        lses = [ld[0, pl.ds(t0, PERM_T), :]]
        for w, d in enumerate(DILATIONS):
            if d == 1:
                continue
            sub_len = seq // d
            chunk = PERM_T // d
            starts = [pl.multiple_of(r * sub_len + i * chunk, chunk) for r in range(d)]
            o_cat = jnp.concatenate([od[w, pl.ds(s, chunk), :] for s in starts], axis=0)
            l_cat = jnp.concatenate([ld[w, pl.ds(s, chunk), :] for s in starts], axis=0)
            l_hi = l_cat.astype(_BF)
            l_lo = (l_cat - l_hi.astype(_F32)).astype(_BF)
            pt = perms_t[d]
            outs.append(jnp.dot(pt, o_cat.astype(_BF), preferred_element_type=_F32))
            lses.append(jnp.dot(pt, l_hi, preferred_element_type=_F32)
                        + jnp.dot(pt, l_lo, preferred_element_type=_F32))
        top = jnp.maximum(jnp.maximum(lses[0], lses[1]), lses[2])
        wts = [jnp.exp(l - top) for l in lses]
        num = wts[0] * outs[0] + wts[1] * outs[1] + wts[2] * outs[2]
        o_ref[0, pl.ds(t0, PERM_T), :] = (num / (wts[0] + wts[1] + wts[2])).astype(o_ref.dtype)
        return carry

    lax.fori_loop(0, n_tiles, merge, 0)


def _attention(qkv, batch, seq):
    qkv = qkv.reshape(batch, seq, 3 * ATT_W)
    n_pairs = ATT_W // LANES
    pad_rows = seq + max(DILATIONS) * 2 * ATT_HALF
    blk = (1, seq, LANES)
    out = pl.pallas_call(
        functools.partial(_att_kernel, seq=seq),
        out_shape=jax.ShapeDtypeStruct((batch, seq, ATT_W), _BF),
        grid=(batch, n_pairs),
        in_specs=[pl.BlockSpec(blk, lambda b, p: (b, 0, p)),
                  pl.BlockSpec(blk, lambda b, p: (b, 0, n_pairs + p)),
                  pl.BlockSpec(blk, lambda b, p: (b, 0, 2 * n_pairs + p))],
        out_specs=pl.BlockSpec(blk, lambda b, p: (b, 0, p)),
        scratch_shapes=[pltpu.VMEM((seq, LANES), _BF)]
        + [pltpu.VMEM((pad_rows, LANES), _BF)] * 4
        + [pltpu.VMEM((len(DILATIONS), seq, LANES), _F32)] * 2
        + [pltpu.VMEM((2, 4, ATT_TQ, ATT_TK), _F32)],
        compiler_params=_params(2),
        name="attention",
    )(qkv, qkv, qkv)
    return out.reshape(batch * seq, ATT_W)


def _ret_kernel(logit_ref, q_ref, k_ref, v_ref, g_ref, o_ref, sf, sb, *, seq):
    head = pl.program_id(1)
    c = RET_CHUNK
    n_chunks = seq // c
    scale = RET_HEAD_DIM ** -0.5

    def log_sigmoid(x):
        return jnp.minimum(x, 0.0) - jnp.log(1.0 + jnp.exp(-jnp.abs(x)))

    lg_f = log_sigmoid(jnp.full((c, LANES), logit_ref[0, head], _F32))
    lg_b = log_sigmoid(jnp.full((c, LANES), logit_ref[1, head], _F32))
    pos = lax.broadcasted_iota(jnp.int32, (c, LANES), 0).astype(_F32)
    col = lax.broadcasted_iota(jnp.int32, (c, LANES), 1).astype(_F32)
    kw_f = jnp.exp(lg_f * (c - 1.0 - pos)) * scale
    kw_b = jnp.exp(lg_b * pos) * scale
    qw_f = jnp.exp(lg_f * (pos + 1.0))
    qw_b = jnp.exp(lg_b * (c - pos))
    g_f = jnp.exp(lg_f * c)
    g_b = jnp.exp(lg_b * c)
    rel = pos - col
    decay = jnp.where(rel >= 0, jnp.exp(lg_f * jnp.maximum(rel, 0.0)),
                      jnp.exp(lg_b * jnp.maximum(-rel, 0.0))) * scale
    lane = lax.broadcasted_iota(jnp.int32, (1, LANES), 1)
    real = lane < RET_HEAD_DIM

    def increments(i, carry):
        t0 = pl.multiple_of(i * c, c)
        k = k_ref[0, pl.ds(t0, c), :].astype(_F32)
        v = v_ref[0, pl.ds(t0, c), :]
        sf[i] = lax.dot_general((k * kw_f).astype(_BF), v, _TN, preferred_element_type=_F32)
        sb[i] = lax.dot_general((k * kw_b).astype(_BF), v, _TN, preferred_element_type=_F32)
        return carry

    lax.fori_loop(0, n_chunks, increments, 0)

    def scan_f(i, state):
        inc = sf[i]
        sf[i] = state
        return state * g_f + inc

    def scan_b(j, state):
        i = n_chunks - 1 - j
        inc = sb[i]
        sb[i] = state
        return state * g_b + inc

    lax.fori_loop(0, n_chunks, scan_f, jnp.zeros((LANES, LANES), _F32))
    lax.fori_loop(0, n_chunks, scan_b, jnp.zeros((LANES, LANES), _F32))

    def outputs(i, carry):
        t0 = pl.multiple_of(i * c, c)
        q = q_ref[0, pl.ds(t0, c), :]
        k = k_ref[0, pl.ds(t0, c), :]
        v = v_ref[0, pl.ds(t0, c), :]
        qf = q.astype(_F32)
        scores = lax.dot_general(q, k, _NT, preferred_element_type=_F32) * decay
        o = jnp.dot(scores.astype(_BF), v, preferred_element_type=_F32)
        o = o + jnp.dot((qf * qw_f).astype(_BF), sf[i].astype(_BF), preferred_element_type=_F32)
        o = o + jnp.dot((qf * qw_b).astype(_BF), sb[i].astype(_BF), preferred_element_type=_F32)
        mu = jnp.sum(o, axis=-1, keepdims=True) * (1.0 / RET_HEAD_DIM)
        dev = jnp.where(real, o - mu, 0.0)
        var = jnp.sum(dev * dev, axis=-1, keepdims=True) * (1.0 / RET_HEAD_DIM)
        g = g_ref[0, pl.ds(t0, c), :]
        o_ref[0, pl.ds(t0, c), :] = (dev * lax.rsqrt(var + NORM_EPS) * (g / (1.0 + jnp.exp(-g)))).astype(o_ref.dtype)
        return carry

    lax.fori_loop(0, n_chunks, outputs, 0)


def _retention(qkv, gate, logits, batch, seq):
    qkv = qkv.reshape(batch, seq, _IN_RET)
    gate = gate.reshape(batch, seq, RET_PAD_W)
    blk = (1, seq, LANES)
    n_chunks = seq // RET_CHUNK
    out = pl.pallas_call(
        functools.partial(_ret_kernel, seq=seq),
        out_shape=jax.ShapeDtypeStruct((batch, seq, RET_PAD_W), _BF),
        grid=(batch, RET_HEADS),
        in_specs=[pl.BlockSpec(memory_space=pltpu.SMEM),
                  pl.BlockSpec(blk, lambda b, h: (b, 0, 3 * h)),
                  pl.BlockSpec(blk, lambda b, h: (b, 0, 3 * h + 1)),
                  pl.BlockSpec(blk, lambda b, h: (b, 0, 3 * h + 2)),
                  pl.BlockSpec(blk, lambda b, h: (b, 0, h))],
        out_specs=pl.BlockSpec(blk, lambda b, h: (b, 0, h)),
        scratch_shapes=[pltpu.VMEM((n_chunks, LANES, LANES), _F32)] * 2,
        compiler_params=_params(2),
        name="retention",
    )(logits, qkv, qkv, qkv, gate)
    return out.reshape(batch * seq, RET_PAD_W)


def _outproj_kernel(att_ref, ret_ref, b_ref, u_ref, up_ref, un_ref, cw_ref, wa_ref, wr_ref, wc_ref,
                    gain_ref, x_ref, o_ref, *, tiles_per_seq):
    tm = u_ref.shape[0]
    t = pl.program_id(0) % tiles_per_seq
    u = u_ref[...]
    prev_row = jnp.where(t == 0, 0.0, up_ref[7:8, :])
    next_row = jnp.where(t == tiles_per_seq - 1, 0.0, un_ref[0:1, :])
    rows = lax.broadcasted_iota(jnp.int32, u.shape, 0)
    u_prev = jnp.where(rows == 0, prev_row, pltpu.roll(u, 1, 0))
    u_next = jnp.where(rows == tm - 1, next_row, pltpu.roll(u, tm - 1, 0))
    conv = cw_ref[0:1, :] * u_prev + cw_ref[1:2, :] * u + cw_ref[2:3, :] * u_next
    cv = (b_ref[...] * conv).astype(_BF)
    y = jnp.dot(att_ref[...], wa_ref[...], preferred_element_type=_F32)
    y = y + jnp.dot(ret_ref[...], wr_ref[...], preferred_element_type=_F32)
    y = y + jnp.dot(cv, wc_ref[...], preferred_element_type=_F32)
    o_ref[...] = x_ref[...] + _rms(y, gain_ref[...])


def _outproj(att, ret, b, u, conv_w, wa, wr, wc, gain, x, seq):
    n = x.shape[0]
    tm = TOKEN_TILE
    sub = 8
    row = lambda i: (i, 0)
    prev_blk = lambda i: (jnp.maximum(i * (tm // sub) - 1, 0), 0)
    next_blk = lambda i: (jnp.minimum((i + 1) * (tm // sub), n // sub - 1), 0)
    return pl.pallas_call(
        functools.partial(_outproj_kernel, tiles_per_seq=seq // tm),
        out_shape=jax.ShapeDtypeStruct(x.shape, x.dtype),
        grid=(n // tm,),
        in_specs=[pl.BlockSpec((tm, ATT_W), row), pl.BlockSpec((tm, RET_PAD_W), row),
                  pl.BlockSpec((tm, CONV_W), row), pl.BlockSpec((tm, CONV_W), row),
                  pl.BlockSpec((sub, CONV_W), prev_blk), pl.BlockSpec((sub, CONV_W), next_blk),
                  _resident((3, CONV_W)), _resident((ATT_W, D_MODEL)), _resident((RET_PAD_W, D_MODEL)),
                  _resident((CONV_W, D_MODEL)), _resident((1, D_MODEL)),
                  pl.BlockSpec((tm, D_MODEL), row)],
        out_specs=pl.BlockSpec((tm, D_MODEL), row),
        compiler_params=_params(1),
        name="outproj",
    )(att, ret, b, u, u, u, conv_w, wa, wr, wc, gain, x)


def _pad_heads(w, axis):
    shape = w.shape[:axis] + (RET_HEADS, RET_HEAD_DIM) + w.shape[axis + 1:]
    pad = [(0, 0)] * (len(shape))
    pad[axis + 1] = (0, LANES - RET_HEAD_DIM)
    w = jnp.pad(w.reshape(shape), pad)
    return w.reshape(w.shape[:axis] + (RET_PAD_W,) + w.shape[axis + 2:])


def _mix_in_weights(w):
    att = w[:, :_IN_ATT]
    c0 = _IN_ATT
    parts = [_pad_heads(w[:, c0 + j * RET_W:c0 + (j + 1) * RET_W], 1).reshape(D_MODEL, RET_HEADS, LANES)
             for j in range(4)]
    qkv = jnp.stack(parts[:3], axis=2).reshape(D_MODEL, _IN_RET)
    gate = parts[3].reshape(D_MODEL, _IN_GATE)
    conv = w[:, c0 + 4 * RET_W:]
    return jnp.concatenate([att, qkv, gate, conv], axis=1).astype(_BF)


def _trunk(x, batch, seq, weights):
    for layer in weights:
        x = _ffn(x, layer["gain"][0:2], *layer["ffn1"])
        att_qkv, ret_qkv, gate, b, u = _inproj(x, layer["gain"][2:3], layer["w_in"])
        att = _attention(att_qkv, batch, seq)
        ret = _retention(ret_qkv, gate, layer["logit"], batch, seq)
        x = _outproj(att, ret, b, u, layer["conv_w"], *layer["w_out"], layer["gain"][3:4], x, seq)
        x = _ffn(x, layer["gain"][4:6], *layer["ffn2"])
    return x


def kernel(x_prompt, x_sample, norm_gain, ffn1_w_gate, ffn1_w_up, ffn1_w_down, w_mix_in, conv_w,
           ret_decay_logit, w_mix_out, ffn2_w_gate, ffn2_w_up, ffn2_w_down):
    weights = []
    for l in range(DEPTH):
        wo = w_mix_out[l]
        weights.append(dict(
            gain=norm_gain[l],
            ffn1=(ffn1_w_gate[l].astype(_BF), ffn1_w_up[l].astype(_BF), ffn1_w_down[l].astype(_BF)),
            ffn2=(ffn2_w_gate[l].astype(_BF), ffn2_w_up[l].astype(_BF), ffn2_w_down[l].astype(_BF)),
            w_in=_mix_in_weights(w_mix_in[l]),
            conv_w=conv_w[l],
            logit=ret_decay_logit[l],
            w_out=(wo[:ATT_W].astype(_BF), _pad_heads(wo[ATT_W:ATT_W + RET_W], 0).astype(_BF),
                   wo[ATT_W + RET_W:].astype(_BF)),
        ))
    outs = []
    for x in (x_prompt, x_sample):
        batch, seq, _ = x.shape
        outs.append(_trunk(x.reshape(batch * seq, D_MODEL), batch, seq, weights).reshape(x.shape))
    return tuple(outs)
```

```python
import functools
import math

import jax
import jax.numpy as jnp
import numpy as np
from jax import lax
from jax.experimental import pallas as pl
from jax.experimental.pallas import tpu as pltpu

D_MODEL = 1024
D_FF = 2816
DEPTH = 2
HEAD_DIM = 64
ATT_HEADS = 6
ATT_W = ATT_HEADS * HEAD_DIM
DILATIONS = (1, 4, 16)
ATT_HALF = 64
RET_HEADS = 4
RET_HEAD_DIM = 96
RET_W = RET_HEADS * RET_HEAD_DIM
RET_CHUNK = 128
CONV_W = 256
NORM_EPS = 1e-6
NEG_INF = -1e30

LANES = 128
RET_PAD_W = RET_HEADS * LANES
ATT_TQ = 128
ATT_TK = ATT_TQ + 2 * ATT_HALF
FF_CHUNK = 256
TOKEN_TILE = 512
VMEM_LIMIT = 56 * 1024 * 1024

_BF = jnp.bfloat16
_F32 = jnp.float32
_NT = (((1,), (1,)), ((), ()))
_TN = (((0,), (0,)), ((), ()))


def _alibi_slope_list(n):
    def pow2(m):
        start = 2.0 ** (-8.0 / m)
        return [start ** (i + 1) for i in range(m)]
    if math.log2(n).is_integer():
        return pow2(n)
    c = 2 ** math.floor(math.log2(n))
    return pow2(c) + _alibi_slope_list(2 * c)[0::2][: n - c]


_SLOPES = [float(np.float32(s)) for s in _alibi_slope_list(ATT_HEADS)]


def _aligned(x, m):
    return x if isinstance(x, int) else pl.multiple_of(x, m)


def _rms(x, g):
    return x * lax.rsqrt(jnp.mean(x * x, axis=-1, keepdims=True) + NORM_EPS) * g


def _params(n_axes):
    return pltpu.CompilerParams(dimension_semantics=("arbitrary",) * n_axes,
                                vmem_limit_bytes=VMEM_LIMIT)


def _resident(shape):
    return pl.BlockSpec(shape, lambda *_: (0,) * len(shape), pipeline_mode=pl.Buffered(1))


def _ffn_kernel(x_ref, gain_ref, wg_ref, wu_ref, wd_ref, o_ref):
    x = x_ref[...]
    h = _rms(x, gain_ref[0:1, :]).astype(_BF)
    acc = jnp.zeros(x.shape, _F32)
    for c in range(D_FF // FF_CHUNK):
        sl = slice(c * FF_CHUNK, (c + 1) * FF_CHUNK)
        g = jnp.dot(h, wg_ref[:, sl], preferred_element_type=_F32)
        u = jnp.dot(h, wu_ref[:, sl], preferred_element_type=_F32)
        a = (g / (1.0 + jnp.exp(-g)) * u).astype(_BF)
        acc = acc + jnp.dot(a, wd_ref[sl, :], preferred_element_type=_F32)
    o_ref[...] = x + 0.5 * _rms(acc, gain_ref[1:2, :])


def _ffn(x, gains, wg, wu, wd):
    n = x.shape[0]
    tm = TOKEN_TILE
    row = lambda i: (i, 0)
    return pl.pallas_call(
        _ffn_kernel,
        out_shape=jax.ShapeDtypeStruct(x.shape, x.dtype),
        grid=(n // tm,),
        in_specs=[pl.BlockSpec((tm, D_MODEL), row), _resident((2, D_MODEL)),
                  _resident((D_MODEL, D_FF)), _resident((D_MODEL, D_FF)), _resident((D_FF, D_MODEL))],
        out_specs=pl.BlockSpec((tm, D_MODEL), row),
        compiler_params=_params(1),
        name="ffn",
    )(x, gains, wg, wu, wd)


_IN_ATT = 3 * ATT_W
_IN_RET = RET_HEADS * 3 * LANES
_IN_GATE = RET_PAD_W
_IN_CONV = 3 * CONV_W
_IN_TOTAL = _IN_ATT + _IN_RET + _IN_GATE + _IN_CONV


def _inproj_kernel(x_ref, gain_ref, w_ref, att_ref, ret_ref, gate_ref, b_ref, u_ref):
    h = _rms(x_ref[...], gain_ref[...]).astype(_BF)
    c0 = 0
    za = jnp.dot(h, w_ref[:, c0:c0 + _IN_ATT], preferred_element_type=_F32)
    att_ref[:, :ATT_W] = za[:, :ATT_W] * (HEAD_DIM ** -0.5)
    att_ref[:, ATT_W:] = za[:, ATT_W:]
    c0 += _IN_ATT
    ret_ref[...] = jnp.dot(h, w_ref[:, c0:c0 + _IN_RET], preferred_element_type=_F32).astype(_BF)
    c0 += _IN_RET
    gate_ref[...] = jnp.dot(h, w_ref[:, c0:c0 + _IN_GATE], preferred_element_type=_F32)
    c0 += _IN_GATE
    zc = jnp.dot(h, w_ref[:, c0:c0 + _IN_CONV], preferred_element_type=_F32)
    b_ref[...] = zc[:, :CONV_W]
    u_ref[...] = zc[:, CONV_W:2 * CONV_W] * zc[:, 2 * CONV_W:]


def _inproj(x, gain, w):
    n = x.shape[0]
    tm = TOKEN_TILE
    row = lambda i: (i, 0)
    widths = (_IN_ATT, _IN_RET, _IN_GATE, CONV_W, CONV_W)
    dtypes = (_F32, _BF, _F32, _F32, _F32)
    return pl.pallas_call(
        _inproj_kernel,
        out_shape=[jax.ShapeDtypeStruct((n, w_), d_) for w_, d_ in zip(widths, dtypes)],
        grid=(n // tm,),
        in_specs=[pl.BlockSpec((tm, D_MODEL), row), _resident((1, D_MODEL)),
                  _resident((D_MODEL, _IN_TOTAL))],
        out_specs=[pl.BlockSpec((tm, w_), row) for w_ in widths],
        compiler_params=_params(1),
        name="inproj",
    )(x, gain, w)


ATT_GROUP = 2
REGROUP_ROWS = 128


def _att_kernel(q_ref, k_ref, v_ref, o_ref, qa, qb, kd, va, vb, od, ld, bias_ref, s_buf, p_buf, m_buf,
                *, seq):
    pair = pl.program_id(1)
    lane = lax.broadcasted_iota(jnp.int32, (1, LANES), 1)
    head_a = lane < HEAD_DIM

    mask_a = jnp.broadcast_to(jnp.where(head_a, 1.0, 0.0), (ATT_TK, LANES)).astype(_BF)
    mask_b = jnp.broadcast_to(jnp.where(head_a, 0.0, 1.0), (ATT_TK, LANES)).astype(_BF)

    def slope(j):
        s = jnp.float32(_SLOPES[j])
        for p in range(1, ATT_HEADS // 2):
            s = jnp.where(pair == p, jnp.float32(_SLOPES[2 * p + j]), s)
        return s

    for w, d in enumerate(DILATIONS):
        sub_len = seq // d
        pad_len = sub_len + 2 * ATT_HALF
        n_blocks = sub_len // ATT_TQ
        n_chunks = sub_len // REGROUP_ROWS

        qi = lax.broadcasted_iota(jnp.int32, (ATT_TQ, ATT_TK), 0)
        kc = lax.broadcasted_iota(jnp.int32, (ATT_TQ, ATT_TK), 1)
        rel = jnp.abs(qi + ATT_HALF - kc)
        dist = (d * rel).astype(_F32)
        for j in range(2):
            sl = slope(j)
            for variant in range(4):
                valid = rel <= ATT_HALF
                if variant & 1:
                    valid = valid & (kc >= ATT_HALF)
                if variant & 2:
                    valid = valid & (kc < ATT_TK - ATT_HALF)
                bias_ref[j, variant] = jnp.where(valid, -sl * dist, NEG_INF)

        kv = w % 2

        def zero_pads(r, carry):
            lo = pl.multiple_of(r * pad_len, ATT_HALF)
            hi = pl.multiple_of(r * pad_len + pad_len - ATT_HALF, ATT_HALF)
            zeros = jnp.zeros((ATT_HALF, LANES), _BF)
            for buf in (kd, va, vb):
                buf[kv, pl.ds(lo, ATT_HALF), :] = zeros
                buf[kv, pl.ds(hi, ATT_HALF), :] = zeros
            return carry

        lax.fori_loop(0, d, zero_pads, 0)

        def regroup(i, carry):
            r = i // n_chunks
            c0 = (i - r * n_chunks) * REGROUP_ROWS
            src = pl.ds(r + d * c0, REGROUP_ROWS, stride=d) if d > 1 else pl.ds(c0, REGROUP_ROWS)
            dq = pl.ds(pl.multiple_of(r * sub_len + c0, REGROUP_ROWS), REGROUP_ROWS)
            dk = pl.ds(pl.multiple_of(r * pad_len + ATT_HALF + c0, ATT_HALF), REGROUP_ROWS)
            qt = q_ref[0, src, :]
            vt = v_ref[0, src, :]
            qa[dq, :] = jnp.where(head_a, qt, 0.0).astype(_BF)
            qb[dq, :] = jnp.where(head_a, 0.0, qt).astype(_BF)
            kd[kv, dk, :] = k_ref[0, src, :].astype(_BF)
            va[kv, dk, :] = jnp.where(head_a, vt, 0.0).astype(_BF)
            vb[kv, dk, :] = jnp.where(head_a, 0.0, vt).astype(_BF)
            return carry

        lax.fori_loop(0, d * n_chunks, regroup, 0)

        def coords(blk):
            r = blk // n_blocks
            b = blk - r * n_blocks
            k0 = _aligned(blk * ATT_TQ + r * (2 * ATT_HALF), ATT_TQ)
            return r, b, k0

        def score(grp, slot):
            for u in range(ATT_GROUP):
                blk = grp * ATT_GROUP + u
                _, b, k0 = coords(blk)
                variant = jnp.where(b == 0, 1, 0) + jnp.where(b == n_blocks - 1, 2, 0)
                rows = pl.ds(_aligned(blk * ATT_TQ, ATT_TQ), ATT_TQ)
                q2 = jnp.concatenate([qa[rows, :], qb[rows, :]], axis=0)
                s = lax.dot_general(q2, kd[kv, pl.ds(k0, ATT_TK), :], _NT, preferred_element_type=_F32)
                for j in range(2):
                    s_buf[slot, u, j] = s[j * ATT_TQ:(j + 1) * ATT_TQ] + bias_ref[j, variant]

        def softmax(slot):
            for u in range(ATT_GROUP):
                tops = [jnp.max(s_buf[slot, u, j], axis=-1, keepdims=True) for j in range(2)]
                m_buf[slot, u] = jnp.where(head_a, tops[0], tops[1])
                for j in range(2):
                    p_buf[slot, u, j] = jnp.exp(s_buf[slot, u, j] - tops[j]).astype(_BF)

        def output(grp, slot):
            for u in range(ATT_GROUP):
                blk = grp * ATT_GROUP + u
                r, b, k0 = coords(blk)
                rhs_a = jnp.concatenate([va[kv, pl.ds(k0, ATT_TK), :], mask_a], axis=1)
                rhs_b = jnp.concatenate([vb[kv, pl.ds(k0, ATT_TK), :], mask_b], axis=1)
                acc = jnp.dot(p_buf[slot, u, 0], rhs_a, preferred_element_type=_F32)
                acc = acc + jnp.dot(p_buf[slot, u, 1], rhs_b, preferred_element_type=_F32)
                den = acc[:, LANES:]
                if d > 1:
                    dst = pl.ds(r + d * b * ATT_TQ, ATT_TQ, stride=d)
                else:
                    dst = pl.ds(_aligned(blk * ATT_TQ, ATT_TQ), ATT_TQ)
                od[w, dst, :] = acc[:, :LANES] / den
                ld[w, dst, :] = m_buf[slot, u] + jnp.log(den)

        n_groups = d * n_blocks // ATT_GROUP

        score(0, 0)
        softmax(0)
        score(1, 1)

        def steady(t, carry):
            for slot in range(2):
                g = 2 * t + slot
                output(g, slot)
                softmax(1 - slot)
                score(g + 2, slot)
            return carry

        lax.fori_loop(0, n_groups // 2 - 1, steady, 0)
        output(n_groups - 2, 0)
        softmax(1)
        output(n_groups - 1, 1)

    def merge(i, carry):
        rows = pl.ds(pl.multiple_of(i * ATT_TQ, ATT_TQ), ATT_TQ)
        lses = [ld[w, rows, :] for w in range(len(DILATIONS))]
        top = jnp.maximum(jnp.maximum(lses[0], lses[1]), lses[2])
        wts = [jnp.exp(l - top) for l in lses]
        num = wts[0] * od[0, rows, :] + wts[1] * od[1, rows, :] + wts[2] * od[2, rows, :]
        o_ref[0, rows, :] = (num / (wts[0] + wts[1] + wts[2])).astype(o_ref.dtype)
        return carry

    lax.fori_loop(0, seq // ATT_TQ, merge, 0)


def _attention(qkv, batch, seq):
    qkv = qkv.reshape(batch, seq, 3 * ATT_W)
    n_pairs = ATT_W // LANES
    pad_rows = seq + max(DILATIONS) * 2 * ATT_HALF
    blk = (1, seq, LANES)
    out = pl.pallas_call(
        functools.partial(_att_kernel, seq=seq),
        out_shape=jax.ShapeDtypeStruct((batch, seq, ATT_W), _BF),
        grid=(batch, n_pairs),
        in_specs=[pl.BlockSpec(blk, lambda b, p: (b, 0, p)),
                  pl.BlockSpec(blk, lambda b, p: (b, 0, n_pairs + p)),
                  pl.BlockSpec(blk, lambda b, p: (b, 0, 2 * n_pairs + p))],
        out_specs=pl.BlockSpec(blk, lambda b, p: (b, 0, p)),
        scratch_shapes=[pltpu.VMEM((seq, LANES), _BF)] * 2
        + [pltpu.VMEM((2, pad_rows, LANES), _BF)] * 3
        + [pltpu.VMEM((len(DILATIONS), seq, LANES), _F32)] * 2
        + [pltpu.VMEM((2, 4, ATT_TQ, ATT_TK), _F32),
           pltpu.VMEM((2, ATT_GROUP, 2, ATT_TQ, ATT_TK), _F32),
           pltpu.VMEM((2, ATT_GROUP, 2, ATT_TQ, ATT_TK), _BF),
           pltpu.VMEM((2, ATT_GROUP, ATT_TQ, LANES), _F32)],
        compiler_params=_params(2),
        name="attention",
    )(qkv, qkv, qkv)
    return out.reshape(batch * seq, ATT_W)


RET_UNROLL = 4


def _ret_kernel(logit_ref, q_ref, k_ref, v_ref, g_ref, o_ref, sf, sb, *, seq):
    head = pl.program_id(1)
    c = RET_CHUNK
    n_chunks = seq // c
    scale = RET_HEAD_DIM ** -0.5

    def log_sigmoid(x):
        return jnp.minimum(x, 0.0) - jnp.log(1.0 + jnp.exp(-jnp.abs(x)))

    lg_f = log_sigmoid(jnp.full((c, LANES), logit_ref[0, head], _F32))
    lg_b = log_sigmoid(jnp.full((c, LANES), logit_ref[1, head], _F32))
    pos = lax.broadcasted_iota(jnp.int32, (c, LANES), 0).astype(_F32)
    col = lax.broadcasted_iota(jnp.int32, (c, LANES), 1).astype(_F32)
    kw_f = jnp.exp(lg_f * (c - 1.0 - pos)) * scale
    kw_b = jnp.exp(lg_b * pos) * scale
    qw_f = jnp.exp(lg_f * (pos + 1.0))
    qw_b = jnp.exp(lg_b * (c - pos))
    g_f = jnp.exp(lg_f * c)
    g_b = jnp.exp(lg_b * c)
    rel = pos - col
    decay = jnp.where(rel >= 0, jnp.exp(lg_f * jnp.maximum(rel, 0.0)),
                      jnp.exp(lg_b * jnp.maximum(-rel, 0.0))) * scale
    lane = lax.broadcasted_iota(jnp.int32, (1, LANES), 1)
    real = lane < RET_HEAD_DIM

    def increments(it, carry):
        for u in range(RET_UNROLL):
            i = it * RET_UNROLL + u
            t0 = pl.multiple_of(i * c, c)
            k = k_ref[0, pl.ds(t0, c), :].astype(_F32)
            v = v_ref[0, pl.ds(t0, c), :]
            sf[i] = lax.dot_general((k * kw_f).astype(_BF), v, _TN, preferred_element_type=_F32)
            sb[i] = lax.dot_general((k * kw_b).astype(_BF), v, _TN, preferred_element_type=_F32)
        return carry

    lax.fori_loop(0, n_chunks // RET_UNROLL, increments, 0)

    def scan_f(i, state):
        inc = sf[i]
        sf[i] = state
        return state * g_f + inc

    def scan_b(j, state):
        i = n_chunks - 1 - j
        inc = sb[i]
        sb[i] = state
        return state * g_b + inc

    lax.fori_loop(0, n_chunks, scan_f, jnp.zeros((LANES, LANES), _F32))
    lax.fori_loop(0, n_chunks, scan_b, jnp.zeros((LANES, LANES), _F32))

    def chunk_output(i):
        t0 = pl.multiple_of(i * c, c)
        q = q_ref[0, pl.ds(t0, c), :]
        k = k_ref[0, pl.ds(t0, c), :]
        v = v_ref[0, pl.ds(t0, c), :]
        qf = q.astype(_F32)
        scores = lax.dot_general(q, k, _NT, preferred_element_type=_F32) * decay
        o = jnp.dot(scores.astype(_BF), v, preferred_element_type=_F32)
        o = o + jnp.dot((qf * qw_f).astype(_BF), sf[i].astype(_BF), preferred_element_type=_F32)
        o = o + jnp.dot((qf * qw_b).astype(_BF), sb[i].astype(_BF), preferred_element_type=_F32)
        mu = jnp.sum(o, axis=-1, keepdims=True) * (1.0 / RET_HEAD_DIM)
        dev = jnp.where(real, o - mu, 0.0)
        var = jnp.sum(dev * dev, axis=-1, keepdims=True) * (1.0 / RET_HEAD_DIM)
        g = g_ref[0, pl.ds(t0, c), :]
        o_ref[0, pl.ds(t0, c), :] = (dev * lax.rsqrt(var + NORM_EPS) * (g / (1.0 + jnp.exp(-g)))).astype(o_ref.dtype)

    def outputs(it, carry):
        for u in range(RET_UNROLL):
            chunk_output(it * RET_UNROLL + u)
        return carry

    lax.fori_loop(0, n_chunks // RET_UNROLL, outputs, 0)


def _retention(qkv, gate, logits, batch, seq):
    qkv = qkv.reshape(batch, seq, _IN_RET)
    gate = gate.reshape(batch, seq, RET_PAD_W)
    blk = (1, seq, LANES)
    n_chunks = seq // RET_CHUNK
    out = pl.pallas_call(
        functools.partial(_ret_kernel, seq=seq),
        out_shape=jax.ShapeDtypeStruct((batch, seq, RET_PAD_W), _BF),
        grid=(batch, RET_HEADS),
        in_specs=[pl.BlockSpec(memory_space=pltpu.SMEM),
                  pl.BlockSpec(blk, lambda b, h: (b, 0, 3 * h)),
                  pl.BlockSpec(blk, lambda b, h: (b, 0, 3 * h + 1)),
                  pl.BlockSpec(blk, lambda b, h: (b, 0, 3 * h + 2)),
                  pl.BlockSpec(blk, lambda b, h: (b, 0, h))],
        out_specs=pl.BlockSpec(blk, lambda b, h: (b, 0, h)),
        scratch_shapes=[pltpu.VMEM((n_chunks, LANES, LANES), _F32)] * 2,
        compiler_params=_params(2),
        name="retention",
    )(logits, qkv, qkv, qkv, gate)
    return out.reshape(batch * seq, RET_PAD_W)


def _outproj_kernel(att_ref, ret_ref, b_ref, u_ref, up_ref, un_ref, cw_ref, wa_ref, wr_ref, wc_ref,
                    gain_ref, x_ref, o_ref, *, tiles_per_seq):
    tm = u_ref.shape[0]
    t = pl.program_id(0) % tiles_per_seq
    u = u_ref[...]
    prev_row = jnp.where(t == 0, 0.0, up_ref[7:8, :])
    next_row = jnp.where(t == tiles_per_seq - 1, 0.0, un_ref[0:1, :])
    rows = lax.broadcasted_iota(jnp.int32, u.shape, 0)
    u_prev = jnp.where(rows == 0, prev_row, pltpu.roll(u, 1, 0))
    u_next = jnp.where(rows == tm - 1, next_row, pltpu.roll(u, tm - 1, 0))
    conv = cw_ref[0:1, :] * u_prev + cw_ref[1:2, :] * u + cw_ref[2:3, :] * u_next
    cv = (b_ref[...] * conv).astype(_BF)
    y = jnp.dot(att_ref[...], wa_ref[...], preferred_element_type=_F32)
    y = y + jnp.dot(ret_ref[...], wr_ref[...], preferred_element_type=_F32)
    y = y + jnp.dot(cv, wc_ref[...], preferred_element_type=_F32)
    o_ref[...] = x_ref[...] + _rms(y, gain_ref[...])


def _outproj(att, ret, b, u, conv_w, wa, wr, wc, gain, x, seq):
    n = x.shape[0]
    tm = TOKEN_TILE
    sub = 8
    row = lambda i: (i, 0)
    prev_blk = lambda i: (jnp.maximum(i * (tm // sub) - 1, 0), 0)
    next_blk = lambda i: (jnp.minimum((i + 1) * (tm // sub), n // sub - 1), 0)
    return pl.pallas_call(
        functools.partial(_outproj_kernel, tiles_per_seq=seq // tm),
        out_shape=jax.ShapeDtypeStruct(x.shape, x.dtype),
        grid=(n // tm,),
        in_specs=[pl.BlockSpec((tm, ATT_W), row), pl.BlockSpec((tm, RET_PAD_W), row),
                  pl.BlockSpec((tm, CONV_W), row), pl.BlockSpec((tm, CONV_W), row),
                  pl.BlockSpec((sub, CONV_W), prev_blk), pl.BlockSpec((sub, CONV_W), next_blk),
                  _resident((3, CONV_W)), _resident((ATT_W, D_MODEL)), _resident((RET_PAD_W, D_MODEL)),
                  _resident((CONV_W, D_MODEL)), _resident((1, D_MODEL)),
                  pl.BlockSpec((tm, D_MODEL), row)],
        out_specs=pl.BlockSpec((tm, D_MODEL), row),
        compiler_params=_params(1),
        name="outproj",
    )(att, ret, b, u, u, u, conv_w, wa, wr, wc, gain, x)


def _pad_heads(w, axis):
    shape = w.shape[:axis] + (RET_HEADS, RET_HEAD_DIM) + w.shape[axis + 1:]
    pad = [(0, 0)] * (len(shape))
    pad[axis + 1] = (0, LANES - RET_HEAD_DIM)
    w = jnp.pad(w.reshape(shape), pad)
    return w.reshape(w.shape[:axis] + (RET_PAD_W,) + w.shape[axis + 2:])


def _mix_in_weights(w):
    att = w[:, :_IN_ATT]
    c0 = _IN_ATT
    parts = [_pad_heads(w[:, c0 + j * RET_W:c0 + (j + 1) * RET_W], 1).reshape(D_MODEL, RET_HEADS, LANES)
             for j in range(4)]
    qkv = jnp.stack(parts[:3], axis=2).reshape(D_MODEL, _IN_RET)
    gate = parts[3].reshape(D_MODEL, _IN_GATE)
    conv = w[:, c0 + 4 * RET_W:]
    return jnp.concatenate([att, qkv, gate, conv], axis=1).astype(_BF)


def _trunk(x, batch, seq, weights):
    for layer in weights:
        x = _ffn(x, layer["gain"][0:2], *layer["ffn1"])
        att_qkv, ret_qkv, gate, b, u = _inproj(x, layer["gain"][2:3], layer["w_in"])
        att = _attention(att_qkv, batch, seq)
        ret = _retention(ret_qkv, gate, layer["logit"], batch, seq)
        x = _outproj(att, ret, b, u, layer["conv_w"], *layer["w_out"], layer["gain"][3:4], x, seq)
        x = _ffn(x, layer["gain"][4:6], *layer["ffn2"])
    return x


def kernel(x_prompt, x_sample, norm_gain, ffn1_w_gate, ffn1_w_up, ffn1_w_down, w_mix_in, conv_w,
           ret_decay_logit, w_mix_out, ffn2_w_gate, ffn2_w_up, ffn2_w_down):
    weights = []
    for l in range(DEPTH):
        wo = w_mix_out[l]
        weights.append(dict(
            gain=norm_gain[l],
            ffn1=(ffn1_w_gate[l].astype(_BF), ffn1_w_up[l].astype(_BF), ffn1_w_down[l].astype(_BF)),
            ffn2=(ffn2_w_gate[l].astype(_BF), ffn2_w_up[l].astype(_BF), ffn2_w_down[l].astype(_BF)),
            w_in=_mix_in_weights(w_mix_in[l]),
            conv_w=conv_w[l],
            logit=ret_decay_logit[l],
            w_out=(wo[:ATT_W].astype(_BF), _pad_heads(wo[ATT_W:ATT_W + RET_W], 0).astype(_BF),
                   wo[ATT_W + RET_W:].astype(_BF)),
        ))
    outs = []
    for x in (x_prompt, x_sample):
        batch, seq, _ = x.shape
        outs.append(_trunk(x.reshape(batch * seq, D_MODEL), batch, seq, weights).reshape(x.shape))
    return tuple(outs)
```

```python
import functools
import math

import jax
import jax.numpy as jnp
import numpy as np
from jax import lax
from jax.experimental import pallas as pl
from jax.experimental.pallas import tpu as pltpu

D_MODEL = 1024
D_FF = 2816
DEPTH = 2
HEAD_DIM = 64
ATT_HEADS = 6
ATT_W = ATT_HEADS * HEAD_DIM
DILATIONS = (1, 4, 16)
ATT_HALF = 64
RET_HEADS = 4
RET_HEAD_DIM = 96
RET_W = RET_HEADS * RET_HEAD_DIM
RET_CHUNK = 128
CONV_W = 256
NORM_EPS = 1e-6
NEG_INF = -1e30

LANES = 128
RET_PAD_W = RET_HEADS * LANES
ATT_TQ = 128
ATT_TK = ATT_TQ + 2 * ATT_HALF
FF_CHUNK = 256
TOKEN_TILE = 512
FFN_SUBTILE = 512
FFN_SUBTILES = 2
VMEM_LIMIT = 56 * 1024 * 1024

_BF = jnp.bfloat16
_F32 = jnp.float32
_NT = (((1,), (1,)), ((), ()))
_TN = (((0,), (0,)), ((), ()))


def _alibi_slope_list(n):
    def pow2(m):
        start = 2.0 ** (-8.0 / m)
        return [start ** (i + 1) for i in range(m)]
    if math.log2(n).is_integer():
        return pow2(n)
    c = 2 ** math.floor(math.log2(n))
    return pow2(c) + _alibi_slope_list(2 * c)[0::2][: n - c]


_SLOPES = [float(np.float32(s)) for s in _alibi_slope_list(ATT_HEADS)]


def _aligned(x, m):
    return x if isinstance(x, int) else pl.multiple_of(x, m)


def _rms(x, g):
    return x * lax.rsqrt(jnp.mean(x * x, axis=-1, keepdims=True) + NORM_EPS) * g


def _params(n_axes):
    return pltpu.CompilerParams(dimension_semantics=("arbitrary",) * n_axes,
                                vmem_limit_bytes=VMEM_LIMIT)


def _resident(shape):
    return pl.BlockSpec(shape, lambda *_: (0,) * len(shape), pipeline_mode=pl.Buffered(1))


def _ffn_kernel(x_ref, gain_ref, wg_ref, wu_ref, wd_ref, o_ref):
    for t in range(x_ref.shape[0] // FFN_SUBTILE):
        rows = slice(t * FFN_SUBTILE, (t + 1) * FFN_SUBTILE)
        x = x_ref[rows, :]
        h = _rms(x, gain_ref[0:1, :]).astype(_BF)
        acc = jnp.zeros(x.shape, _F32)
        for c in range(D_FF // FF_CHUNK):
            sl = slice(c * FF_CHUNK, (c + 1) * FF_CHUNK)
            g = jnp.dot(h, wg_ref[:, sl], preferred_element_type=_F32)
            u = jnp.dot(h, wu_ref[:, sl], preferred_element_type=_F32)
            a = (g / (1.0 + jnp.exp(-g)) * u).astype(_BF)
            acc = acc + jnp.dot(a, wd_ref[sl, :], preferred_element_type=_F32)
        o_ref[rows, :] = x + 0.5 * _rms(acc, gain_ref[1:2, :])


def _ffn(x, gains, wg, wu, wd):
    n = x.shape[0]
    tm = FFN_SUBTILES * FFN_SUBTILE
    row = lambda i: (i, 0)
    return pl.pallas_call(
        _ffn_kernel,
        out_shape=jax.ShapeDtypeStruct(x.shape, x.dtype),
        grid=(n // tm,),
        in_specs=[pl.BlockSpec((tm, D_MODEL), row), _resident((2, D_MODEL)),
                  _resident((D_MODEL, D_FF)), _resident((D_MODEL, D_FF)), _resident((D_FF, D_MODEL))],
        out_specs=pl.BlockSpec((tm, D_MODEL), row),
        compiler_params=_params(1),
        name="ffn",
    )(x, gains, wg, wu, wd)


_IN_ATT = 3 * ATT_W
_IN_RET = RET_HEADS * 3 * LANES
_IN_GATE = RET_PAD_W
_IN_CONV = 3 * CONV_W
_IN_TOTAL = _IN_ATT + _IN_RET + _IN_GATE + _IN_CONV
IN_CHUNK = 256


def _inproj_kernel(x_ref, gain_ref, w_ref, att1_ref, att4_ref, att16_ref, ret_ref, gate_ref, b_ref, u_ref,
                   stage, stage4):
    tm = x_ref.shape[0]
    h = _rms(x_ref[...], gain_ref[...]).astype(_BF)
    c0 = 0
    def project(start, width):
        for c in range(0, width, IN_CHUNK):
            cw = min(IN_CHUNK, width - c)
            yield c, jnp.dot(h, w_ref[:, start + c:start + c + cw], preferred_element_type=_F32)

    n_slabs = _IN_ATT // LANES
    for c, z in project(c0, _IN_ATT):
        for s in range(c // LANES, (c + z.shape[1]) // LANES):
            zs = z[:, s * LANES - c:(s + 1) * LANES - c]
            if (s + 1) * LANES <= ATT_W:
                zs = zs * (HEAD_DIM ** -0.5)
            stage[s] = zs
            att1_ref[:, s * LANES:(s + 1) * LANES] = zs.astype(_BF)
    d4, d16 = DILATIONS[1], DILATIONS[2]
    rows4, rows16 = tm // d4, tm // d16
    for s in range(n_slabs):
        cols = slice(s * LANES, (s + 1) * LANES)
        for r4 in range(d4):
            x = stage[s, pl.ds(r4, rows4, stride=d4), :]
            att4_ref[0, r4, :, cols] = x.astype(_BF)
            stage4[s, r4 * rows4:(r4 + 1) * rows4, :] = x
        for r4 in range(d4):
            for j in range(d16 // d4):
                x = stage4[s, pl.ds(r4 * rows4 + j, rows16, stride=d16 // d4), :]
                att16_ref[0, r4 + d4 * j, :, cols] = x.astype(_BF)
    c0 += _IN_ATT
    for c, z in project(c0, _IN_RET):
        ret_ref[:, c:c + z.shape[1]] = z.astype(_BF)
    c0 += _IN_RET
    for c, z in project(c0, _IN_GATE):
        gate_ref[:, c:c + z.shape[1]] = z
    c0 += _IN_GATE
    (_, zb), (_, zc), (_, zu) = project(c0, _IN_CONV)
    b_ref[...] = zb
    u_ref[...] = zc * zu


def _inproj(x, gain, w, batch, seq):
    n = x.shape[0]
    tm = TOKEN_TILE
    tiles_per_seq = seq // tm
    row = lambda i: (i, 0)
    widths = (_IN_ATT, _IN_RET, _IN_GATE, CONV_W, CONV_W)
    dtypes = (_BF, _BF, _F32, _F32, _F32)
    flat = [(jax.ShapeDtypeStruct((n, w_), d_), pl.BlockSpec((tm, w_), row)) for w_, d_ in zip(widths, dtypes)]
    grouped = [(jax.ShapeDtypeStruct((batch, d, seq // d, _IN_ATT), _BF),
                pl.BlockSpec((1, d, tm // d, _IN_ATT), lambda i: (i // tiles_per_seq, 0, i % tiles_per_seq, 0)))
               for d in DILATIONS[1:]]
    outs = [flat[0]] + grouped + flat[1:]
    att1, att4, att16, ret, gate, b, u = pl.pallas_call(
        _inproj_kernel,
        out_shape=[o[0] for o in outs],
        grid=(n // tm,),
        in_specs=[pl.BlockSpec((tm, D_MODEL), row), _resident((1, D_MODEL)),
                  _resident((D_MODEL, _IN_TOTAL))],
        out_specs=[o[1] for o in outs],
        scratch_shapes=[pltpu.VMEM((_IN_ATT // LANES, tm, LANES), _F32)] * 2,
        compiler_params=_params(1),
        name="inproj",
    )(x, gain, w)
    att = [a.reshape(batch, seq, _IN_ATT) for a in (att1, att4, att16)]
    return att, ret, gate, b, u


ATT_GROUP = 2
ATT_VARIANTS = 6


def _att_kernel(*refs, seq):
    n_win = len(DILATIONS)
    qkv_refs = [refs[3 * w:3 * w + 3] for w in range(n_win)]
    o_ref, od, ld, bias_ref, s_buf, p_buf, m_buf = refs[3 * n_win:]
    pair = pl.program_id(1)
    lane = lax.broadcasted_iota(jnp.int32, (1, LANES), 1)
    head_a = lane < HEAD_DIM
    zero_bf = jnp.zeros((), _BF)

    mask_a = jnp.broadcast_to(jnp.where(head_a, 1.0, 0.0), (ATT_TK, LANES)).astype(_BF)
    mask_b = jnp.broadcast_to(jnp.where(head_a, 0.0, 1.0), (ATT_TK, LANES)).astype(_BF)

    def heads(x):
        return jnp.where(head_a, x, zero_bf), jnp.where(head_a, zero_bf, x)

    def slope(j):
        s = jnp.float32(_SLOPES[j])
        for p in range(1, ATT_HEADS // 2):
            s = jnp.where(pair == p, jnp.float32(_SLOPES[2 * p + j]), s)
        return s

    for w, d in enumerate(DILATIONS):
        q_ref, k_ref, v_ref = qkv_refs[w]
        sub_len = seq // d
        n_blocks = sub_len // ATT_TQ
        n_total = d * n_blocks

        qi = lax.broadcasted_iota(jnp.int32, (ATT_TQ, ATT_TK), 0)
        kc = lax.broadcasted_iota(jnp.int32, (ATT_TQ, ATT_TK), 1)
        single = n_blocks == 1
        variants = {3: (ATT_HALF, True, True)} if single else {
            0: (ATT_HALF, False, False), 1: (ATT_HALF, True, False), 2: (ATT_HALF, False, True)}
        variants[4] = (0, True, single)
        variants[5] = (2 * ATT_HALF, single, True)
        for variant, (shift, first, last) in variants.items():
            rel = jnp.abs(qi + shift - kc)
            valid = rel <= ATT_HALF
            if first:
                valid = valid & (kc >= shift)
            if last:
                valid = valid & (kc < shift + ATT_TQ)
            dist = (d * rel).astype(_F32)
            for j in range(2):
                bias_ref[j, variant] = jnp.where(valid, -slope(j) * dist, NEG_INF)

        def coords(blk):
            r = blk // n_blocks
            b = blk - r * n_blocks
            if isinstance(blk, int):
                at_start, at_end = int(blk == 0), int(blk == n_total - 1)
                variant = 4 if at_start else 5 if at_end else int(b == 0) + 2 * int(b == n_blocks - 1)
            else:
                at_start, at_end = jnp.where(blk == 0, 1, 0), jnp.where(blk == n_total - 1, 1, 0)
                edge = jnp.where(b == 0, 1, 0) + jnp.where(b == n_blocks - 1, 2, 0)
                variant = jnp.where(blk == 0, 4, jnp.where(blk == n_total - 1, 5, edge))
            q0 = _aligned(blk * ATT_TQ, ATT_TQ)
            t0 = _aligned(blk * ATT_TQ - ATT_HALF + ATT_HALF * (at_start - at_end), ATT_HALF)
            return r, b, q0, t0, variant

        def score(grp, slot):
            for u in range(ATT_GROUP):
                _, _, q0, t0, variant = coords(grp * ATT_GROUP + u)
                q2 = jnp.concatenate(heads(q_ref[0, pl.ds(q0, ATT_TQ), :]), axis=0)
                s = lax.dot_general(q2, k_ref[0, pl.ds(t0, ATT_TK), :], _NT, preferred_element_type=_F32)
                for j in range(2):
                    s_buf[slot, u, j] = s[j * ATT_TQ:(j + 1) * ATT_TQ] + bias_ref[j, variant]

        def softmax(slot):
            for u in range(ATT_GROUP):
                tops = [jnp.max(s_buf[slot, u, j], axis=-1, keepdims=True) for j in range(2)]
                m_buf[slot, u] = jnp.where(head_a, tops[0], tops[1])
                for j in range(2):
                    p_buf[slot, u, j] = jnp.exp(s_buf[slot, u, j] - tops[j]).astype(_BF)

        def output(grp, slot):
            for u in range(ATT_GROUP):
                r, b, q0, t0, _ = coords(grp * ATT_GROUP + u)
                v_a, v_b = heads(v_ref[0, pl.ds(t0, ATT_TK), :])
                acc = jnp.dot(p_buf[slot, u, 0], jnp.concatenate([v_a, mask_a], axis=1),
                              preferred_element_type=_F32)
                acc = acc + jnp.dot(p_buf[slot, u, 1], jnp.concatenate([v_b, mask_b], axis=1),
                                    preferred_element_type=_F32)
                den = acc[:, LANES:]
                dst = pl.ds(r + d * b * ATT_TQ, ATT_TQ, stride=d) if d > 1 else pl.ds(q0, ATT_TQ)
                od[w, dst, :] = acc[:, :LANES] / den
                ld[w, dst, :] = m_buf[slot, u] + jnp.log(den)

        n_groups = d * n_blocks // ATT_GROUP

        score(0, 0)
        softmax(0)
        score(1, 1)

        def steady(t, carry):
            for slot in range(2):
                g = 2 * t + slot
                output(g, slot)
                softmax(1 - slot)
                score(g + 2, slot)
            return carry

        lax.fori_loop(0, n_groups // 2 - 1, steady, 0)
        output(n_groups - 2, 0)
        softmax(1)
        output(n_groups - 1, 1)

    def merge(i, carry):
        rows = pl.ds(pl.multiple_of(i * ATT_TQ, ATT_TQ), ATT_TQ)
        lses = [ld[w, rows, :] for w in range(len(DILATIONS))]
        top = jnp.maximum(jnp.maximum(lses[0], lses[1]), lses[2])
        wts = [jnp.exp(l - top) for l in lses]
        num = wts[0] * od[0, rows, :] + wts[1] * od[1, rows, :] + wts[2] * od[2, rows, :]
        o_ref[0, rows, :] = (num / (wts[0] + wts[1] + wts[2])).astype(o_ref.dtype)
        return carry

    lax.fori_loop(0, seq // ATT_TQ, merge, 0)


def _attention(qkv_by_dilation, batch, seq):
    n_pairs = ATT_W // LANES
    blk = (1, seq, LANES)
    part = lambda j: pl.BlockSpec(blk, lambda b, p: (b, 0, j * n_pairs + p))
    out = pl.pallas_call(
        functools.partial(_att_kernel, seq=seq),
        out_shape=jax.ShapeDtypeStruct((batch, seq, ATT_W), _BF),
        grid=(batch, n_pairs),
        in_specs=[part(j) for _ in DILATIONS for j in range(3)],
        out_specs=part(0),
        scratch_shapes=[pltpu.VMEM((len(DILATIONS), seq, LANES), _F32)] * 2
        + [pltpu.VMEM((2, ATT_VARIANTS, ATT_TQ, ATT_TK), _F32),
           pltpu.VMEM((2, ATT_GROUP, 2, ATT_TQ, ATT_TK), _F32),
           pltpu.VMEM((2, ATT_GROUP, 2, ATT_TQ, ATT_TK), _BF),
           pltpu.VMEM((2, ATT_GROUP, ATT_TQ, LANES), _F32)],
        compiler_params=_params(2),
        name="attention",
    )(*[a for a in qkv_by_dilation for _ in range(3)])
    return out.reshape(batch * seq, ATT_W)


RET_UNROLL = 8


def _ret_kernel(logit_ref, q_ref, k_ref, v_ref, g_ref, o_ref, sf, sb, *, seq):
    head = pl.program_id(1)
    c = RET_CHUNK
    n_chunks = seq // c
    scale = RET_HEAD_DIM ** -0.5

    def log_sigmoid(x):
        return jnp.minimum(x, 0.0) - jnp.log(1.0 + jnp.exp(-jnp.abs(x)))

    lg_f = log_sigmoid(jnp.full((c, LANES), logit_ref[0, head], _F32))
    lg_b = log_sigmoid(jnp.full((c, LANES), logit_ref[1, head], _F32))
    pos = lax.broadcasted_iota(jnp.int32, (c, LANES), 0).astype(_F32)
    col = lax.broadcasted_iota(jnp.int32, (c, LANES), 1).astype(_F32)
    kw_f = jnp.exp(lg_f * (c - 1.0 - pos)) * scale
    kw_b = jnp.exp(lg_b * pos) * scale
    qw_f = jnp.exp(lg_f * (pos + 1.0))
    qw_b = jnp.exp(lg_b * (c - pos))
    g_f = jnp.exp(lg_f * c)
    g_b = jnp.exp(lg_b * c)
    rel = pos - col
    decay = jnp.where(rel >= 0, jnp.exp(lg_f * jnp.maximum(rel, 0.0)),
                      jnp.exp(lg_b * jnp.maximum(-rel, 0.0))) * scale
    lane = lax.broadcasted_iota(jnp.int32, (1, LANES), 1)
    real = lane < RET_HEAD_DIM

    def increments(it, carry):
        for u in range(RET_UNROLL):
            i = it * RET_UNROLL + u
            t0 = pl.multiple_of(i * c, c)
            k = k_ref[0, pl.ds(t0, c), :].astype(_F32)
            v = v_ref[0, pl.ds(t0, c), :]
            sf[i] = lax.dot_general((k * kw_f).astype(_BF), v, _TN, preferred_element_type=_F32)
            sb[i] = lax.dot_general((k * kw_b).astype(_BF), v, _TN, preferred_element_type=_F32)
        return carry

    lax.fori_loop(0, n_chunks // RET_UNROLL, increments, 0)

    def scan_f(i, state):
        inc = sf[i]
        sf[i] = state
        return state * g_f + inc

    def scan_b(j, state):
        i = n_chunks - 1 - j
        inc = sb[i]
        sb[i] = state
        return state * g_b + inc

    lax.fori_loop(0, n_chunks, scan_f, jnp.zeros((LANES, LANES), _F32))
    lax.fori_loop(0, n_chunks, scan_b, jnp.zeros((LANES, LANES), _F32))

    def chunk_output(i):
        t0 = pl.multiple_of(i * c, c)
        q = q_ref[0, pl.ds(t0, c), :]
        k = k_ref[0, pl.ds(t0, c), :]
        v = v_ref[0, pl.ds(t0, c), :]
        qf = q.astype(_F32)
        scores = lax.dot_general(q, k, _NT, preferred_element_type=_F32) * decay
        o = jnp.dot(scores.astype(_BF), v, preferred_element_type=_F32)
        o = o + jnp.dot((qf * qw_f).astype(_BF), sf[i].astype(_BF), preferred_element_type=_F32)
        o = o + jnp.dot((qf * qw_b).astype(_BF), sb[i].astype(_BF), preferred_element_type=_F32)
        mu = jnp.sum(o, axis=-1, keepdims=True) * (1.0 / RET_HEAD_DIM)
        dev = jnp.where(real, o - mu, 0.0)
        var = jnp.sum(dev * dev, axis=-1, keepdims=True) * (1.0 / RET_HEAD_DIM)
        g = g_ref[0, pl.ds(t0, c), :]
        o_ref[0, pl.ds(t0, c), :] = (dev * lax.rsqrt(var + NORM_EPS) * (g / (1.0 + jnp.exp(-g)))).astype(o_ref.dtype)

    def outputs(it, carry):
        for u in range(RET_UNROLL):
            chunk_output(it * RET_UNROLL + u)
        return carry

    lax.fori_loop(0, n_chunks // RET_UNROLL, outputs, 0)


def _retention(qkv, gate, logits, batch, seq):
    qkv = qkv.reshape(batch, seq, _IN_RET)
    gate = gate.reshape(batch, seq, RET_PAD_W)
    blk = (1, seq, LANES)
    n_chunks = seq // RET_CHUNK
    out = pl.pallas_call(
        functools.partial(_ret_kernel, seq=seq),
        out_shape=jax.ShapeDtypeStruct((batch, seq, RET_PAD_W), _BF),
        grid=(batch, RET_HEADS),
        in_specs=[pl.BlockSpec(memory_space=pltpu.SMEM),
                  pl.BlockSpec(blk, lambda b, h: (b, 0, 3 * h)),
                  pl.BlockSpec(blk, lambda b, h: (b, 0, 3 * h + 1)),
                  pl.BlockSpec(blk, lambda b, h: (b, 0, 3 * h + 2)),
                  pl.BlockSpec(blk, lambda b, h: (b, 0, h))],
        out_specs=pl.BlockSpec(blk, lambda b, h: (b, 0, h)),
        scratch_shapes=[pltpu.VMEM((n_chunks, LANES, LANES), _F32)] * 2,
        compiler_params=_params(2),
        name="retention",
    )(logits, qkv, qkv, qkv, gate)
    return out.reshape(batch * seq, RET_PAD_W)


def _outproj_kernel(att_ref, ret_ref, b_ref, u_ref, up_ref, un_ref, cw_ref, wa_ref, wr_ref, wc_ref,
                    gain_ref, x_ref, o_ref, *, tiles_per_seq):
    tm = u_ref.shape[0]
    t = pl.program_id(0) % tiles_per_seq
    u = u_ref[...]
    prev_row = jnp.where(t == 0, 0.0, up_ref[7:8, :])
    next_row = jnp.where(t == tiles_per_seq - 1, 0.0, un_ref[0:1, :])
    rows = lax.broadcasted_iota(jnp.int32, u.shape, 0)
    u_prev = jnp.where(rows == 0, prev_row, pltpu.roll(u, 1, 0))
    u_next = jnp.where(rows == tm - 1, next_row, pltpu.roll(u, tm - 1, 0))
    conv = cw_ref[0:1, :] * u_prev + cw_ref[1:2, :] * u + cw_ref[2:3, :] * u_next
    cv = (b_ref[...] * conv).astype(_BF)
    y = jnp.dot(att_ref[...], wa_ref[...], preferred_element_type=_F32)
    y = y + jnp.dot(ret_ref[...], wr_ref[...], preferred_element_type=_F32)
    y = y + jnp.dot(cv, wc_ref[...], preferred_element_type=_F32)
    o_ref[...] = x_ref[...] + _rms(y, gain_ref[...])


def _outproj(att, ret, b, u, conv_w, wa, wr, wc, gain, x, seq):
    n = x.shape[0]
    tm = TOKEN_TILE
    sub = 8
    row = lambda i: (i, 0)
    prev_blk = lambda i: (jnp.maximum(i * (tm // sub) - 1, 0), 0)
    next_blk = lambda i: (jnp.minimum((i + 1) * (tm // sub), n // sub - 1), 0)
    return pl.pallas_call(
        functools.partial(_outproj_kernel, tiles_per_seq=seq // tm),
        out_shape=jax.ShapeDtypeStruct(x.shape, x.dtype),
        grid=(n // tm,),
        in_specs=[pl.BlockSpec((tm, ATT_W), row), pl.BlockSpec((tm, RET_PAD_W), row),
                  pl.BlockSpec((tm, CONV_W), row), pl.BlockSpec((tm, CONV_W), row),
                  pl.BlockSpec((sub, CONV_W), prev_blk), pl.BlockSpec((sub, CONV_W), next_blk),
                  _resident((3, CONV_W)), _resident((ATT_W, D_MODEL)), _resident((RET_PAD_W, D_MODEL)),
                  _resident((CONV_W, D_MODEL)), _resident((1, D_MODEL)),
                  pl.BlockSpec((tm, D_MODEL), row)],
        out_specs=pl.BlockSpec((tm, D_MODEL), row),
        compiler_params=_params(1),
        name="outproj",
    )(att, ret, b, u, u, u, conv_w, wa, wr, wc, gain, x)


def _pad_heads(w, axis):
    shape = w.shape[:axis] + (RET_HEADS, RET_HEAD_DIM) + w.shape[axis + 1:]
    pad = [(0, 0)] * (len(shape))
    pad[axis + 1] = (0, LANES - RET_HEAD_DIM)
    w = jnp.pad(w.reshape(shape), pad)
    return w.reshape(w.shape[:axis] + (RET_PAD_W,) + w.shape[axis + 2:])


def _mix_in_weights(w):
    att = w[:, :_IN_ATT]
    c0 = _IN_ATT
    parts = [_pad_heads(w[:, c0 + j * RET_W:c0 + (j + 1) * RET_W], 1).reshape(D_MODEL, RET_HEADS, LANES)
             for j in range(4)]
    qkv = jnp.stack(parts[:3], axis=2).reshape(D_MODEL, _IN_RET)
    gate = parts[3].reshape(D_MODEL, _IN_GATE)
    conv = w[:, c0 + 4 * RET_W:]
    return jnp.concatenate([att, qkv, gate, conv], axis=1).astype(_BF)


def _trunk(x, batch, seq, weights):
    for layer in weights:
        x = _ffn(x, layer["gain"][0:2], *layer["ffn1"])
        att_qkv, ret_qkv, gate, b, u = _inproj(x, layer["gain"][2:3], layer["w_in"], batch, seq)
        att = _attention(att_qkv, batch, seq)
        ret = _retention(ret_qkv, gate, layer["logit"], batch, seq)
        x = _outproj(att, ret, b, u, layer["conv_w"], *layer["w_out"], layer["gain"][3:4], x, seq)
        x = _ffn(x, layer["gain"][4:6], *layer["ffn2"])
    return x


def kernel(x_prompt, x_sample, norm_gain, ffn1_w_gate, ffn1_w_up, ffn1_w_down, w_mix_in, conv_w,
           ret_decay_logit, w_mix_out, ffn2_w_gate, ffn2_w_up, ffn2_w_down):
    weights = []
    for l in range(DEPTH):
        wo = w_mix_out[l]
        weights.append(dict(
            gain=norm_gain[l],
            ffn1=(ffn1_w_gate[l].astype(_BF), ffn1_w_up[l].astype(_BF), ffn1_w_down[l].astype(_BF)),
            ffn2=(ffn2_w_gate[l].astype(_BF), ffn2_w_up[l].astype(_BF), ffn2_w_down[l].astype(_BF)),
            w_in=_mix_in_weights(w_mix_in[l]),
            conv_w=conv_w[l],
            logit=ret_decay_logit[l],
            w_out=(wo[:ATT_W].astype(_BF), _pad_heads(wo[ATT_W:ATT_W + RET_W], 0).astype(_BF),
                   wo[ATT_W + RET_W:].astype(_BF)),
        ))
    outs = []
    for x in (x_prompt, x_sample):
        batch, seq, _ = x.shape
        outs.append(_trunk(x.reshape(batch * seq, D_MODEL), batch, seq, weights).reshape(x.shape))
    return tuple(outs)
```

```python
import functools
import math

import jax
import jax.numpy as jnp
import numpy as np
from jax import lax
from jax.experimental import pallas as pl
from jax.experimental.pallas import tpu as pltpu

D_MODEL = 1024
D_FF = 2816
DEPTH = 2
HEAD_DIM = 64
ATT_HEADS = 6
ATT_W = ATT_HEADS * HEAD_DIM
DILATIONS = (1, 4, 16)
ATT_HALF = 64
RET_HEADS = 4
RET_HEAD_DIM = 96
RET_W = RET_HEADS * RET_HEAD_DIM
RET_CHUNK = 128
CONV_W = 256
NORM_EPS = 1e-6
NEG_INF = -1e30

LANES = 128
RET_PAD_W = RET_HEADS * LANES
ATT_TQ = 128
ATT_TK = ATT_TQ + 2 * ATT_HALF
FF_CHUNK = 256
TOKEN_TILE = 512
FFN_SUBTILE = 512
FFN_SUBTILES = 2
VMEM_LIMIT = 56 * 1024 * 1024

_BF = jnp.bfloat16
_F32 = jnp.float32
_NT = (((1,), (1,)), ((), ()))
_TN = (((0,), (0,)), ((), ()))


def _alibi_slope_list(n):
    def pow2(m):
        start = 2.0 ** (-8.0 / m)
        return [start ** (i + 1) for i in range(m)]
    if math.log2(n).is_integer():
        return pow2(n)
    c = 2 ** math.floor(math.log2(n))
    return pow2(c) + _alibi_slope_list(2 * c)[0::2][: n - c]


_SLOPES = [float(np.float32(s)) for s in _alibi_slope_list(ATT_HEADS)]


def _aligned(x, m):
    return x if isinstance(x, int) else pl.multiple_of(x, m)


def _rms(x, g):
    return x * lax.rsqrt(jnp.mean(x * x, axis=-1, keepdims=True) + NORM_EPS) * g


def _params(n_axes):
    return pltpu.CompilerParams(dimension_semantics=("arbitrary",) * n_axes,
                                vmem_limit_bytes=VMEM_LIMIT)


def _resident(shape):
    return pl.BlockSpec(shape, lambda *_: (0,) * len(shape), pipeline_mode=pl.Buffered(1))


def _swiglu_residual(x, gain_ref, wg_ref, wu_ref, wd_ref):
    h = _rms(x, gain_ref[0:1, :]).astype(_BF)
    acc = jnp.zeros(x.shape, _F32)
    for c in range(D_FF // FF_CHUNK):
        sl = slice(c * FF_CHUNK, (c + 1) * FF_CHUNK)
        g = jnp.dot(h, wg_ref[:, sl], preferred_element_type=_F32)
        u = jnp.dot(h, wu_ref[:, sl], preferred_element_type=_F32)
        a = (g / (1.0 + jnp.exp(-g)) * u).astype(_BF)
        acc = acc + jnp.dot(a, wd_ref[sl, :], preferred_element_type=_F32)
    return x + 0.5 * _rms(acc, gain_ref[1:2, :])


def _subtiles(n_rows):
    return [slice(r, r + FFN_SUBTILE) for r in range(0, n_rows, FFN_SUBTILE)]


def _ffn_kernel(x_ref, gain_ref, wg_ref, wu_ref, wd_ref, o_ref):
    for rows in _subtiles(x_ref.shape[0]):
        o_ref[rows, :] = _swiglu_residual(x_ref[rows, :], gain_ref, wg_ref, wu_ref, wd_ref)


def _ffn(x, gains, wg, wu, wd):
    n = x.shape[0]
    tm = FFN_SUBTILES * FFN_SUBTILE
    row = lambda i: (i, 0)
    return pl.pallas_call(
        _ffn_kernel,
        out_shape=jax.ShapeDtypeStruct(x.shape, x.dtype),
        grid=(n // tm,),
        in_specs=[pl.BlockSpec((tm, D_MODEL), row), _resident((2, D_MODEL)),
                  _resident((D_MODEL, D_FF)), _resident((D_MODEL, D_FF)), _resident((D_FF, D_MODEL))],
        out_specs=pl.BlockSpec((tm, D_MODEL), row),
        compiler_params=_params(1),
        name="ffn",
    )(x, gains, wg, wu, wd)


_IN_ATT = 3 * ATT_W
_IN_RET = RET_HEADS * 3 * LANES
_IN_GATE = RET_PAD_W
_IN_CONV = 3 * CONV_W
_IN_TOTAL = _IN_ATT + _IN_RET + _IN_GATE + _IN_CONV
IN_CHUNK = 256


def _inproj_kernel(x_ref, gain_ref, w_ref, att1_ref, att4_ref, att16_ref, ret_ref, gate_ref, b_ref, u_ref,
                   stage, stage4):
    tm = x_ref.shape[0]
    h = _rms(x_ref[...], gain_ref[...]).astype(_BF)
    c0 = 0
    def project(start, width):
        for c in range(0, width, IN_CHUNK):
            cw = min(IN_CHUNK, width - c)
            yield c, jnp.dot(h, w_ref[:, start + c:start + c + cw], preferred_element_type=_F32)

    n_slabs = _IN_ATT // LANES
    for c, z in project(c0, _IN_ATT):
        for s in range(c // LANES, (c + z.shape[1]) // LANES):
            zs = z[:, s * LANES - c:(s + 1) * LANES - c]
            if (s + 1) * LANES <= ATT_W:
                zs = zs * (HEAD_DIM ** -0.5)
            stage[s] = zs
            att1_ref[:, s * LANES:(s + 1) * LANES] = zs.astype(_BF)
    d4, d16 = DILATIONS[1], DILATIONS[2]
    rows4, rows16 = tm // d4, tm // d16
    for s in range(n_slabs):
        cols = slice(s * LANES, (s + 1) * LANES)
        for r4 in range(d4):
            x = stage[s, pl.ds(r4, rows4, stride=d4), :]
            att4_ref[0, r4, :, cols] = x.astype(_BF)
            stage4[s, r4 * rows4:(r4 + 1) * rows4, :] = x
        for r4 in range(d4):
            for j in range(d16 // d4):
                x = stage4[s, pl.ds(r4 * rows4 + j, rows16, stride=d16 // d4), :]
                att16_ref[0, r4 + d4 * j, :, cols] = x.astype(_BF)
    c0 += _IN_ATT
    for c, z in project(c0, _IN_RET):
        ret_ref[:, c:c + z.shape[1]] = z.astype(_BF)
    c0 += _IN_RET
    for c, z in project(c0, _IN_GATE):
        gate_ref[:, c:c + z.shape[1]] = z
    c0 += _IN_GATE
    (_, zb), (_, zc), (_, zu) = project(c0, _IN_CONV)
    b_ref[...] = zb
    u_ref[...] = zc * zu


def _inproj(x, gain, w, batch, seq):
    n = x.shape[0]
    tm = TOKEN_TILE
    tiles_per_seq = seq // tm
    row = lambda i: (i, 0)
    widths = (_IN_ATT, _IN_RET, _IN_GATE, CONV_W, CONV_W)
    dtypes = (_BF, _BF, _F32, _F32, _F32)
    flat = [(jax.ShapeDtypeStruct((n, w_), d_), pl.BlockSpec((tm, w_), row)) for w_, d_ in zip(widths, dtypes)]
    grouped = [(jax.ShapeDtypeStruct((batch, d, seq // d, _IN_ATT), _BF),
                pl.BlockSpec((1, d, tm // d, _IN_ATT), lambda i: (i // tiles_per_seq, 0, i % tiles_per_seq, 0)))
               for d in DILATIONS[1:]]
    outs = [flat[0]] + grouped + flat[1:]
    att1, att4, att16, ret, gate, b, u = pl.pallas_call(
        _inproj_kernel,
        out_shape=[o[0] for o in outs],
        grid=(n // tm,),
        in_specs=[pl.BlockSpec((tm, D_MODEL), row), _resident((1, D_MODEL)),
                  _resident((D_MODEL, _IN_TOTAL))],
        out_specs=[o[1] for o in outs],
        scratch_shapes=[pltpu.VMEM((_IN_ATT // LANES, tm, LANES), _F32)] * 2,
        compiler_params=_params(1),
        name="inproj",
    )(x, gain, w)
    att = [a.reshape(batch, seq, _IN_ATT) for a in (att1, att4, att16)]
    return att, ret, gate, b, u


ATT_GROUP = 2
ATT_VARIANTS = 6


def _att_kernel(*refs, seq):
    n_win = len(DILATIONS)
    qkv_refs = [refs[3 * w:3 * w + 3] for w in range(n_win)]
    o_ref, od, ld, bias_ref, s_buf, p_buf, m_buf = refs[3 * n_win:]
    pair = pl.program_id(1)
    lane = lax.broadcasted_iota(jnp.int32, (1, LANES), 1)
    head_a = lane < HEAD_DIM
    zero_bf = jnp.zeros((), _BF)

    mask_a = jnp.broadcast_to(jnp.where(head_a, 1.0, 0.0), (ATT_TK, LANES)).astype(_BF)
    mask_b = jnp.broadcast_to(jnp.where(head_a, 0.0, 1.0), (ATT_TK, LANES)).astype(_BF)

    def heads(x):
        return jnp.where(head_a, x, zero_bf), jnp.where(head_a, zero_bf, x)

    def slope(j):
        s = jnp.float32(_SLOPES[j])
        for p in range(1, ATT_HEADS // 2):
            s = jnp.where(pair == p, jnp.float32(_SLOPES[2 * p + j]), s)
        return s

    for w, d in enumerate(DILATIONS):
        q_ref, k_ref, v_ref = qkv_refs[w]
        sub_len = seq // d
        n_blocks = sub_len // ATT_TQ
        n_total = d * n_blocks

        qi = lax.broadcasted_iota(jnp.int32, (ATT_TQ, ATT_TK), 0)
        kc = lax.broadcasted_iota(jnp.int32, (ATT_TQ, ATT_TK), 1)
        single = n_blocks == 1
        variants = {3: (ATT_HALF, True, True)} if single else {
            0: (ATT_HALF, False, False), 1: (ATT_HALF, True, False), 2: (ATT_HALF, False, True)}
        variants[4] = (0, True, single)
        variants[5] = (2 * ATT_HALF, single, True)
        for variant, (shift, first, last) in variants.items():
            rel = jnp.abs(qi + shift - kc)
            valid = rel <= ATT_HALF
            if first:
                valid = valid & (kc >= shift)
            if last:
                valid = valid & (kc < shift + ATT_TQ)
            dist = (d * rel).astype(_F32)
            for j in range(2):
                bias_ref[j, variant] = jnp.where(valid, -slope(j) * dist, NEG_INF)

        def coords(blk):
            r = blk // n_blocks
            b = blk - r * n_blocks
            if isinstance(blk, int):
                at_start, at_end = int(blk == 0), int(blk == n_total - 1)
                variant = 4 if at_start else 5 if at_end else int(b == 0) + 2 * int(b == n_blocks - 1)
            else:
                at_start, at_end = jnp.where(blk == 0, 1, 0), jnp.where(blk == n_total - 1, 1, 0)
                edge = jnp.where(b == 0, 1, 0) + jnp.where(b == n_blocks - 1, 2, 0)
                variant = jnp.where(blk == 0, 4, jnp.where(blk == n_total - 1, 5, edge))
            q0 = _aligned(blk * ATT_TQ, ATT_TQ)
            t0 = _aligned(blk * ATT_TQ - ATT_HALF + ATT_HALF * (at_start - at_end), ATT_HALF)
            return r, b, q0, t0, variant

        def score(grp, slot):
            for u in range(ATT_GROUP):
                _, _, q0, t0, variant = coords(grp * ATT_GROUP + u)
                q2 = jnp.concatenate(heads(q_ref[0, pl.ds(q0, ATT_TQ), :]), axis=0)
                s = lax.dot_general(q2, k_ref[0, pl.ds(t0, ATT_TK), :], _NT, preferred_element_type=_F32)
                for j in range(2):
                    s_buf[slot, u, j] = s[j * ATT_TQ:(j + 1) * ATT_TQ] + bias_ref[j, variant]

        def softmax(slot):
            for u in range(ATT_GROUP):
                tops = [jnp.max(s_buf[slot, u, j], axis=-1, keepdims=True) for j in range(2)]
                m_buf[slot, u] = jnp.where(head_a, tops[0], tops[1])
                for j in range(2):
                    p_buf[slot, u, j] = jnp.exp(s_buf[slot, u, j] - tops[j]).astype(_BF)

        def output(grp, slot):
            for u in range(ATT_GROUP):
                r, b, q0, t0, _ = coords(grp * ATT_GROUP + u)
                v_a, v_b = heads(v_ref[0, pl.ds(t0, ATT_TK), :])
                acc = jnp.dot(p_buf[slot, u, 0], jnp.concatenate([v_a, mask_a], axis=1),
                              preferred_element_type=_F32)
                acc = acc + jnp.dot(p_buf[slot, u, 1], jnp.concatenate([v_b, mask_b], axis=1),
                                    preferred_element_type=_F32)
                den = acc[:, LANES:]
                dst = pl.ds(r + d * b * ATT_TQ, ATT_TQ, stride=d) if d > 1 else pl.ds(q0, ATT_TQ)
                od[w, dst, :] = acc[:, :LANES] / den
                ld[w, dst, :] = m_buf[slot, u] + jnp.log(den)

        n_groups = d * n_blocks // ATT_GROUP

        score(0, 0)
        softmax(0)
        score(1, 1)

        def steady(t, carry):
            for slot in range(2):
                g = 2 * t + slot
                output(g, slot)
                softmax(1 - slot)
                score(g + 2, slot)
            return carry

        lax.fori_loop(0, n_groups // 2 - 1, steady, 0)
        output(n_groups - 2, 0)
        softmax(1)
        output(n_groups - 1, 1)

    def merge(i, carry):
        rows = pl.ds(pl.multiple_of(i * ATT_TQ, ATT_TQ), ATT_TQ)
        lses = [ld[w, rows, :] for w in range(len(DILATIONS))]
        top = jnp.maximum(jnp.maximum(lses[0], lses[1]), lses[2])
        wts = [jnp.exp(l - top) for l in lses]
        num = wts[0] * od[0, rows, :] + wts[1] * od[1, rows, :] + wts[2] * od[2, rows, :]
        o_ref[0, rows, :] = (num / (wts[0] + wts[1] + wts[2])).astype(o_ref.dtype)
        return carry

    lax.fori_loop(0, seq // ATT_TQ, merge, 0)


def _attention(qkv_by_dilation, batch, seq):
    n_pairs = ATT_W // LANES
    blk = (1, seq, LANES)
    part = lambda j: pl.BlockSpec(blk, lambda b, p: (b, 0, j * n_pairs + p))
    out = pl.pallas_call(
        functools.partial(_att_kernel, seq=seq),
        out_shape=jax.ShapeDtypeStruct((batch, seq, ATT_W), _BF),
        grid=(batch, n_pairs),
        in_specs=[part(j) for _ in DILATIONS for j in range(3)],
        out_specs=part(0),
        scratch_shapes=[pltpu.VMEM((len(DILATIONS), seq, LANES), _F32)] * 2
        + [pltpu.VMEM((2, ATT_VARIANTS, ATT_TQ, ATT_TK), _F32),
           pltpu.VMEM((2, ATT_GROUP, 2, ATT_TQ, ATT_TK), _F32),
           pltpu.VMEM((2, ATT_GROUP, 2, ATT_TQ, ATT_TK), _BF),
           pltpu.VMEM((2, ATT_GROUP, ATT_TQ, LANES), _F32)],
        compiler_params=_params(2),
        name="attention",
    )(*[a for a in qkv_by_dilation for _ in range(3)])
    return out.reshape(batch * seq, ATT_W)


RET_UNROLL = 8


def _ret_kernel(logit_ref, q_ref, k_ref, v_ref, g_ref, o_ref, sf, sb, *, seq):
    head = pl.program_id(1)
    c = RET_CHUNK
    n_chunks = seq // c
    scale = RET_HEAD_DIM ** -0.5

    def log_sigmoid(x):
        return jnp.minimum(x, 0.0) - jnp.log(1.0 + jnp.exp(-jnp.abs(x)))

    lg_f = log_sigmoid(jnp.full((c, LANES), logit_ref[0, head], _F32))
    lg_b = log_sigmoid(jnp.full((c, LANES), logit_ref[1, head], _F32))
    pos = lax.broadcasted_iota(jnp.int32, (c, LANES), 0).astype(_F32)
    col = lax.broadcasted_iota(jnp.int32, (c, LANES), 1).astype(_F32)
    kw_f = jnp.exp(lg_f * (c - 1.0 - pos)) * scale
    kw_b = jnp.exp(lg_b * pos) * scale
    qw_f = jnp.exp(lg_f * (pos + 1.0))
    qw_b = jnp.exp(lg_b * (c - pos))
    g_f = jnp.exp(lg_f * c)
    g_b = jnp.exp(lg_b * c)
    rel = pos - col
    decay = jnp.where(rel >= 0, jnp.exp(lg_f * jnp.maximum(rel, 0.0)),
                      jnp.exp(lg_b * jnp.maximum(-rel, 0.0))) * scale
    lane = lax.broadcasted_iota(jnp.int32, (1, LANES), 1)
    real = lane < RET_HEAD_DIM

    def increments(it, carry):
        for u in range(RET_UNROLL):
            i = it * RET_UNROLL + u
            t0 = pl.multiple_of(i * c, c)
            k = k_ref[0, pl.ds(t0, c), :].astype(_F32)
            v = v_ref[0, pl.ds(t0, c), :]
            sf[i] = lax.dot_general((k * kw_f).astype(_BF), v, _TN, preferred_element_type=_F32)
            sb[i] = lax.dot_general((k * kw_b).astype(_BF), v, _TN, preferred_element_type=_F32)
        return carry

    lax.fori_loop(0, n_chunks // RET_UNROLL, increments, 0)

    def scan_f(i, state):
        inc = sf[i]
        sf[i] = state
        return state * g_f + inc

    def scan_b(j, state):
        i = n_chunks - 1 - j
        inc = sb[i]
        sb[i] = state
        return state * g_b + inc

    lax.fori_loop(0, n_chunks, scan_f, jnp.zeros((LANES, LANES), _F32))
    lax.fori_loop(0, n_chunks, scan_b, jnp.zeros((LANES, LANES), _F32))

    def chunk_output(i):
        t0 = pl.multiple_of(i * c, c)
        q = q_ref[0, pl.ds(t0, c), :]
        k = k_ref[0, pl.ds(t0, c), :]
        v = v_ref[0, pl.ds(t0, c), :]
        qf = q.astype(_F32)
        scores = lax.dot_general(q, k, _NT, preferred_element_type=_F32) * decay
        o = jnp.dot(scores.astype(_BF), v, preferred_element_type=_F32)
        o = o + jnp.dot((qf * qw_f).astype(_BF), sf[i].astype(_BF), preferred_element_type=_F32)
        o = o + jnp.dot((qf * qw_b).astype(_BF), sb[i].astype(_BF), preferred_element_type=_F32)
        mu = jnp.sum(o, axis=-1, keepdims=True) * (1.0 / RET_HEAD_DIM)
        dev = jnp.where(real, o - mu, 0.0)
        var = jnp.sum(dev * dev, axis=-1, keepdims=True) * (1.0 / RET_HEAD_DIM)
        g = g_ref[0, pl.ds(t0, c), :]
        o_ref[0, pl.ds(t0, c), :] = (dev * lax.rsqrt(var + NORM_EPS) * (g / (1.0 + jnp.exp(-g)))).astype(o_ref.dtype)

    def outputs(it, carry):
        for u in range(RET_UNROLL):
            chunk_output(it * RET_UNROLL + u)
        return carry

    lax.fori_loop(0, n_chunks // RET_UNROLL, outputs, 0)


def _retention(qkv, gate, logits, batch, seq):
    qkv = qkv.reshape(batch, seq, _IN_RET)
    gate = gate.reshape(batch, seq, RET_PAD_W)
    blk = (1, seq, LANES)
    n_chunks = seq // RET_CHUNK
    out = pl.pallas_call(
        functools.partial(_ret_kernel, seq=seq),
        out_shape=jax.ShapeDtypeStruct((batch, seq, RET_PAD_W), _BF),
        grid=(batch, RET_HEADS),
        in_specs=[pl.BlockSpec(memory_space=pltpu.SMEM),
                  pl.BlockSpec(blk, lambda b, h: (b, 0, 3 * h)),
                  pl.BlockSpec(blk, lambda b, h: (b, 0, 3 * h + 1)),
                  pl.BlockSpec(blk, lambda b, h: (b, 0, 3 * h + 2)),
                  pl.BlockSpec(blk, lambda b, h: (b, 0, h))],
        out_specs=pl.BlockSpec(blk, lambda b, h: (b, 0, h)),
        scratch_shapes=[pltpu.VMEM((n_chunks, LANES, LANES), _F32)] * 2,
        compiler_params=_params(2),
        name="retention",
    )(logits, qkv, qkv, qkv, gate)
    return out.reshape(batch * seq, RET_PAD_W)


def _mix_out_ffn_kernel(att_ref, ret_ref, b_ref, u_ref, up_ref, un_ref, cw_ref, wa_ref, wr_ref, wc_ref,
                        mix_gain_ref, x_ref, ffn_gain_ref, wg_ref, wu_ref, wd_ref, o_ref, *, tiles_per_seq):
    tm = u_ref.shape[0]
    t = pl.program_id(0) % tiles_per_seq
    u = u_ref[...]
    prev_row = jnp.where(t == 0, 0.0, up_ref[7:8, :])
    next_row = jnp.where(t == tiles_per_seq - 1, 0.0, un_ref[0:1, :])
    row_id = lax.broadcasted_iota(jnp.int32, u.shape, 0)
    u_prev = jnp.where(row_id == 0, prev_row, pltpu.roll(u, 1, 0))
    u_next = jnp.where(row_id == tm - 1, next_row, pltpu.roll(u, tm - 1, 0))
    conv = cw_ref[0:1, :] * u_prev + cw_ref[1:2, :] * u + cw_ref[2:3, :] * u_next
    cv = (b_ref[...] * conv).astype(_BF)
    for rows in _subtiles(tm):
        y = jnp.dot(att_ref[rows, :], wa_ref[...], preferred_element_type=_F32)
        y = y + jnp.dot(ret_ref[rows, :], wr_ref[...], preferred_element_type=_F32)
        y = y + jnp.dot(cv[rows], wc_ref[...], preferred_element_type=_F32)
        x_mid = x_ref[rows, :] + _rms(y, mix_gain_ref[...])
        o_ref[rows, :] = _swiglu_residual(x_mid, ffn_gain_ref, wg_ref, wu_ref, wd_ref)


def _mix_out_ffn(att, ret, b, u, conv_w, wa, wr, wc, mix_gain, x, ffn_gains, wg, wu, wd, seq):
    n = x.shape[0]
    tm = FFN_SUBTILES * FFN_SUBTILE
    sub = 8
    row = lambda i: (i, 0)
    prev_blk = lambda i: (jnp.maximum(i * (tm // sub) - 1, 0), 0)
    next_blk = lambda i: (jnp.minimum((i + 1) * (tm // sub), n // sub - 1), 0)
    return pl.pallas_call(
        functools.partial(_mix_out_ffn_kernel, tiles_per_seq=seq // tm),
        out_shape=jax.ShapeDtypeStruct(x.shape, x.dtype),
        grid=(n // tm,),
        in_specs=[pl.BlockSpec((tm, ATT_W), row), pl.BlockSpec((tm, RET_PAD_W), row),
                  pl.BlockSpec((tm, CONV_W), row), pl.BlockSpec((tm, CONV_W), row),
                  pl.BlockSpec((sub, CONV_W), prev_blk), pl.BlockSpec((sub, CONV_W), next_blk),
                  _resident((3, CONV_W)), _resident((ATT_W, D_MODEL)), _resident((RET_PAD_W, D_MODEL)),
                  _resident((CONV_W, D_MODEL)), _resident((1, D_MODEL)),
                  pl.BlockSpec((tm, D_MODEL), row), _resident((2, D_MODEL)),
                  _resident((D_MODEL, D_FF)), _resident((D_MODEL, D_FF)), _resident((D_FF, D_MODEL))],
        out_specs=pl.BlockSpec((tm, D_MODEL), row),
        compiler_params=_params(1),
        name="mix_out_ffn",
    )(att, ret, b, u, u, u, conv_w, wa, wr, wc, mix_gain, x, ffn_gains, wg, wu, wd)


def _pad_heads(w, axis):
    shape = w.shape[:axis] + (RET_HEADS, RET_HEAD_DIM) + w.shape[axis + 1:]
    pad = [(0, 0)] * (len(shape))
    pad[axis + 1] = (0, LANES - RET_HEAD_DIM)
    w = jnp.pad(w.reshape(shape), pad)
    return w.reshape(w.shape[:axis] + (RET_PAD_W,) + w.shape[axis + 2:])


def _mix_in_weights(w):
    att = w[:, :_IN_ATT]
    c0 = _IN_ATT
    parts = [_pad_heads(w[:, c0 + j * RET_W:c0 + (j + 1) * RET_W], 1).reshape(D_MODEL, RET_HEADS, LANES)
             for j in range(4)]
    qkv = jnp.stack(parts[:3], axis=2).reshape(D_MODEL, _IN_RET)
    gate = parts[3].reshape(D_MODEL, _IN_GATE)
    conv = w[:, c0 + 4 * RET_W:]
    return jnp.concatenate([att, qkv, gate, conv], axis=1).astype(_BF)


def _trunk(x, batch, seq, weights):
    for layer in weights:
        x = _ffn(x, layer["gain"][0:2], *layer["ffn1"])
        att_qkv, ret_qkv, gate, b, u = _inproj(x, layer["gain"][2:3], layer["w_in"], batch, seq)
        att = _attention(att_qkv, batch, seq)
        ret = _retention(ret_qkv, gate, layer["logit"], batch, seq)
        x = _mix_out_ffn(att, ret, b, u, layer["conv_w"], *layer["w_out"], layer["gain"][3:4], x,
                         layer["gain"][4:6], *layer["ffn2"], seq)
    return x


def kernel(x_prompt, x_sample, norm_gain, ffn1_w_gate, ffn1_w_up, ffn1_w_down, w_mix_in, conv_w,
           ret_decay_logit, w_mix_out, ffn2_w_gate, ffn2_w_up, ffn2_w_down):
    weights = []
    for l in range(DEPTH):
        wo = w_mix_out[l]
        weights.append(dict(
            gain=norm_gain[l],
            ffn1=(ffn1_w_gate[l].astype(_BF), ffn1_w_up[l].astype(_BF), ffn1_w_down[l].astype(_BF)),
            ffn2=(ffn2_w_gate[l].astype(_BF), ffn2_w_up[l].astype(_BF), ffn2_w_down[l].astype(_BF)),
            w_in=_mix_in_weights(w_mix_in[l]),
            conv_w=conv_w[l],
            logit=ret_decay_logit[l],
            w_out=(wo[:ATT_W].astype(_BF), _pad_heads(wo[ATT_W:ATT_W + RET_W], 0).astype(_BF),
                   wo[ATT_W + RET_W:].astype(_BF)),
        ))
    outs = []
    for x in (x_prompt, x_sample):
        batch, seq, _ = x.shape
        outs.append(_trunk(x.reshape(batch * seq, D_MODEL), batch, seq, weights).reshape(x.shape))
    return tuple(outs)
```

```python
import functools
import math

import jax
import jax.numpy as jnp
import numpy as np
from jax import lax
from jax.experimental import pallas as pl
from jax.experimental.pallas import tpu as pltpu

D_MODEL = 1024
D_FF = 2816
DEPTH = 2
HEAD_DIM = 64
ATT_HEADS = 6
ATT_W = ATT_HEADS * HEAD_DIM
DILATIONS = (1, 4, 16)
ATT_HALF = 64
RET_HEADS = 4
RET_HEAD_DIM = 96
RET_W = RET_HEADS * RET_HEAD_DIM
RET_CHUNK = 128
CONV_W = 256
NORM_EPS = 1e-6
NEG_INF = -1e30

LANES = 128
RET_PAD_W = RET_HEADS * LANES
ATT_TQ = 128
ATT_TK = ATT_TQ + 2 * ATT_HALF
FF_CHUNK = 256
TOKEN_TILE = 512
FFN_SUBTILE = 512
FFN_SUBTILES = 2
VMEM_LIMIT = 56 * 1024 * 1024

_BF = jnp.bfloat16
_F32 = jnp.float32
_NT = (((1,), (1,)), ((), ()))
_TN = (((0,), (0,)), ((), ()))


def _alibi_slope_list(n):
    def pow2(m):
        start = 2.0 ** (-8.0 / m)
        return [start ** (i + 1) for i in range(m)]
    if math.log2(n).is_integer():
        return pow2(n)
    c = 2 ** math.floor(math.log2(n))
    return pow2(c) + _alibi_slope_list(2 * c)[0::2][: n - c]


_SLOPES = [float(np.float32(s)) for s in _alibi_slope_list(ATT_HEADS)]


def _aligned(x, m):
    return x if isinstance(x, int) else pl.multiple_of(x, m)


def _rms(x, g):
    return x * lax.rsqrt(jnp.mean(x * x, axis=-1, keepdims=True) + NORM_EPS) * g


def _params(n_axes):
    return pltpu.CompilerParams(dimension_semantics=("arbitrary",) * n_axes,
                                vmem_limit_bytes=VMEM_LIMIT)


def _resident(shape):
    return pl.BlockSpec(shape, lambda *_: (0,) * len(shape), pipeline_mode=pl.Buffered(1))


def _swiglu_residual(x, gain_ref, wg_ref, wu_ref, wd_ref):
    h = _rms(x, gain_ref[0:1, :]).astype(_BF)
    acc = jnp.zeros(x.shape, _F32)
    for c in range(D_FF // FF_CHUNK):
        sl = slice(c * FF_CHUNK, (c + 1) * FF_CHUNK)
        g = jnp.dot(h, wg_ref[:, sl], preferred_element_type=_F32)
        u = jnp.dot(h, wu_ref[:, sl], preferred_element_type=_F32)
        a = (g / (1.0 + jnp.exp(-g)) * u).astype(_BF)
        acc = acc + jnp.dot(a, wd_ref[sl, :], preferred_element_type=_F32)
    return x + 0.5 * _rms(acc, gain_ref[1:2, :])


def _subtiles(n_rows):
    return [slice(r, r + FFN_SUBTILE) for r in range(0, n_rows, FFN_SUBTILE)]


def _ffn_kernel(x_ref, gain_ref, wg_ref, wu_ref, wd_ref, o_ref):
    for rows in _subtiles(x_ref.shape[0]):
        o_ref[rows, :] = _swiglu_residual(x_ref[rows, :], gain_ref, wg_ref, wu_ref, wd_ref)


def _ffn(x, gains, wg, wu, wd):
    n = x.shape[0]
    tm = FFN_SUBTILES * FFN_SUBTILE
    row = lambda i: (i, 0)
    return pl.pallas_call(
        _ffn_kernel,
        out_shape=jax.ShapeDtypeStruct(x.shape, x.dtype),
        grid=(n // tm,),
        in_specs=[pl.BlockSpec((tm, D_MODEL), row), _resident((2, D_MODEL)),
                  _resident((D_MODEL, D_FF)), _resident((D_MODEL, D_FF)), _resident((D_FF, D_MODEL))],
        out_specs=pl.BlockSpec((tm, D_MODEL), row),
        compiler_params=_params(1),
        name="ffn",
    )(x, gains, wg, wu, wd)


_IN_ATT = 3 * ATT_W
_IN_RET = RET_HEADS * 3 * LANES
_IN_GATE = RET_PAD_W
_IN_CONV = 3 * CONV_W
_IN_TOTAL = _IN_ATT + _IN_RET + _IN_GATE + _IN_CONV
IN_CHUNK = 256


def _inproj_kernel(x_ref, gain_ref, w_ref, att1_ref, att4_ref, att16_ref, ret_ref, gate_ref, b_ref, u_ref,
                   stage, stage4):
    tm = x_ref.shape[0]
    h = _rms(x_ref[...], gain_ref[...]).astype(_BF)
    c0 = 0
    def project(start, width):
        for c in range(0, width, IN_CHUNK):
            cw = min(IN_CHUNK, width - c)
            yield c, jnp.dot(h, w_ref[:, start + c:start + c + cw], preferred_element_type=_F32)

    n_slabs = _IN_ATT // LANES
    for c, z in project(c0, _IN_ATT):
        for s in range(c // LANES, (c + z.shape[1]) // LANES):
            zs = z[:, s * LANES - c:(s + 1) * LANES - c]
            if (s + 1) * LANES <= ATT_W:
                zs = zs * (HEAD_DIM ** -0.5)
            stage[s] = zs
            att1_ref[:, s * LANES:(s + 1) * LANES] = zs.astype(_BF)
    d4, d16 = DILATIONS[1], DILATIONS[2]
    rows4, rows16 = tm // d4, tm // d16
    for s in range(n_slabs):
        cols = slice(s * LANES, (s + 1) * LANES)
        for r4 in range(d4):
            x = stage[s, pl.ds(r4, rows4, stride=d4), :]
            att4_ref[0, r4, :, cols] = x.astype(_BF)
            stage4[s, r4 * rows4:(r4 + 1) * rows4, :] = x
        for r4 in range(d4):
            for j in range(d16 // d4):
                x = stage4[s, pl.ds(r4 * rows4 + j, rows16, stride=d16 // d4), :]
                att16_ref[0, r4 + d4 * j, :, cols] = x.astype(_BF)
    c0 += _IN_ATT
    for c, z in project(c0, _IN_RET):
        ret_ref[:, c:c + z.shape[1]] = z.astype(_BF)
    c0 += _IN_RET
    for c, z in project(c0, _IN_GATE):
        gate_ref[:, c:c + z.shape[1]] = z
    c0 += _IN_GATE
    (_, zb), (_, zc), (_, zu) = project(c0, _IN_CONV)
    b_ref[...] = zb
    u_ref[...] = zc * zu


def _inproj(x, gain, w, batch, seq):
    n = x.shape[0]
    tm = TOKEN_TILE
    tiles_per_seq = seq // tm
    row = lambda i: (i, 0)
    widths = (_IN_ATT, _IN_RET, _IN_GATE, CONV_W, CONV_W)
    dtypes = (_BF, _BF, _F32, _F32, _F32)
    flat = [(jax.ShapeDtypeStruct((n, w_), d_), pl.BlockSpec((tm, w_), row)) for w_, d_ in zip(widths, dtypes)]
    grouped = [(jax.ShapeDtypeStruct((batch, d, seq // d, _IN_ATT), _BF),
                pl.BlockSpec((1, d, tm // d, _IN_ATT), lambda i: (i // tiles_per_seq, 0, i % tiles_per_seq, 0)))
               for d in DILATIONS[1:]]
    outs = [flat[0]] + grouped + flat[1:]
    att1, att4, att16, ret, gate, b, u = pl.pallas_call(
        _inproj_kernel,
        out_shape=[o[0] for o in outs],
        grid=(n // tm,),
        in_specs=[pl.BlockSpec((tm, D_MODEL), row), _resident((1, D_MODEL)),
                  _resident((D_MODEL, _IN_TOTAL))],
        out_specs=[o[1] for o in outs],
        scratch_shapes=[pltpu.VMEM((_IN_ATT // LANES, tm, LANES), _F32)] * 2,
        compiler_params=_params(1),
        name="inproj",
    )(x, gain, w)
    att = [a.reshape(batch, seq, _IN_ATT) for a in (att1, att4, att16)]
    return att, ret, gate, b, u


ATT_GROUP = 2
ATT_VARIANTS = 6


def _att_kernel(*refs, seq):
    n_win = len(DILATIONS)
    qkv_refs = [refs[3 * w:3 * w + 3] for w in range(n_win)]
    o_ref, od, ld, bias_ref, s_buf, p_buf, m_buf = refs[3 * n_win:]
    pair = pl.program_id(0)
    lane = lax.broadcasted_iota(jnp.int32, (1, LANES), 1)
    head_a = lane < HEAD_DIM
    zero_bf = jnp.zeros((), _BF)

    mask_a = jnp.broadcast_to(jnp.where(head_a, 1.0, 0.0), (ATT_TK, LANES)).astype(_BF)
    mask_b = jnp.broadcast_to(jnp.where(head_a, 0.0, 1.0), (ATT_TK, LANES)).astype(_BF)

    def heads(x):
        return jnp.where(head_a, x, zero_bf), jnp.where(head_a, zero_bf, x)

    def slope(j):
        s = jnp.float32(_SLOPES[j])
        for p in range(1, ATT_HEADS // 2):
            s = jnp.where(pair == p, jnp.float32(_SLOPES[2 * p + j]), s)
        return s

    @pl.when(pl.program_id(1) == 0)
    def _():
        qi = lax.broadcasted_iota(jnp.int32, (ATT_TQ, ATT_TK), 0)
        kc = lax.broadcasted_iota(jnp.int32, (ATT_TQ, ATT_TK), 1)
        for w, d in enumerate(DILATIONS):
            single = seq // d == ATT_TQ
            variants = {3: (ATT_HALF, True, True)} if single else {
                0: (ATT_HALF, False, False), 1: (ATT_HALF, True, False), 2: (ATT_HALF, False, True)}
            variants[4] = (0, True, single)
            variants[5] = (2 * ATT_HALF, single, True)
            for variant, (shift, first, last) in variants.items():
                rel = jnp.abs(qi + shift - kc)
                valid = rel <= ATT_HALF
                if first:
                    valid = valid & (kc >= shift)
                if last:
                    valid = valid & (kc < shift + ATT_TQ)
                dist = (d * rel).astype(_F32)
                for j in range(2):
                    bias_ref[w, j, variant] = jnp.where(valid, -slope(j) * dist, NEG_INF)

    for w, d in enumerate(DILATIONS):
        q_ref, k_ref, v_ref = qkv_refs[w]
        sub_len = seq // d
        n_blocks = sub_len // ATT_TQ
        n_total = d * n_blocks

        def coords(blk):
            r = blk // n_blocks
            b = blk - r * n_blocks
            if isinstance(blk, int):
                at_start, at_end = int(blk == 0), int(blk == n_total - 1)
                variant = 4 if at_start else 5 if at_end else int(b == 0) + 2 * int(b == n_blocks - 1)
            else:
                at_start, at_end = jnp.where(blk == 0, 1, 0), jnp.where(blk == n_total - 1, 1, 0)
                edge = jnp.where(b == 0, 1, 0) + jnp.where(b == n_blocks - 1, 2, 0)
                variant = jnp.where(blk == 0, 4, jnp.where(blk == n_total - 1, 5, edge))
            q0 = _aligned(blk * ATT_TQ, ATT_TQ)
            t0 = _aligned(blk * ATT_TQ - ATT_HALF + ATT_HALF * (at_start - at_end), ATT_HALF)
            return r, b, q0, t0, variant

        def score(grp, slot):
            for u in range(ATT_GROUP):
                _, _, q0, t0, variant = coords(grp * ATT_GROUP + u)
                q2 = jnp.concatenate(heads(q_ref[0, pl.ds(q0, ATT_TQ), :]), axis=0)
                s = lax.dot_general(q2, k_ref[0, pl.ds(t0, ATT_TK), :], _NT, preferred_element_type=_F32)
                for j in range(2):
                    s_buf[slot, u, j] = s[j * ATT_TQ:(j + 1) * ATT_TQ] + bias_ref[w, j, variant]

        def softmax(slot):
            for u in range(ATT_GROUP):
                tops = [jnp.max(s_buf[slot, u, j], axis=-1, keepdims=True) for j in range(2)]
                m_buf[slot, u] = jnp.where(head_a, tops[0], tops[1])
                for j in range(2):
                    p_buf[slot, u, j] = jnp.exp(s_buf[slot, u, j] - tops[j]).astype(_BF)

        def output(grp, slot):
            for u in range(ATT_GROUP):
                r, b, q0, t0, _ = coords(grp * ATT_GROUP + u)
                v_a, v_b = heads(v_ref[0, pl.ds(t0, ATT_TK), :])
                acc = jnp.dot(p_buf[slot, u, 0], jnp.concatenate([v_a, mask_a], axis=1),
                              preferred_element_type=_F32)
                acc = acc + jnp.dot(p_buf[slot, u, 1], jnp.concatenate([v_b, mask_b], axis=1),
                                    preferred_element_type=_F32)
                den = acc[:, LANES:]
                dst = pl.ds(r + d * b * ATT_TQ, ATT_TQ, stride=d) if d > 1 else pl.ds(q0, ATT_TQ)
                od[w, dst, :] = acc[:, :LANES] / den
                ld[w, dst, :] = m_buf[slot, u] + jnp.log(den)

        n_groups = d * n_blocks // ATT_GROUP

        score(0, 0)
        softmax(0)
        score(1, 1)

        def steady(t, carry):
            for slot in range(2):
                g = 2 * t + slot
                output(g, slot)
                softmax(1 - slot)
                score(g + 2, slot)
            return carry

        lax.fori_loop(0, n_groups // 2 - 1, steady, 0)
        output(n_groups - 2, 0)
        softmax(1)
        output(n_groups - 1, 1)

    def merge(i, carry):
        rows = pl.ds(pl.multiple_of(i * ATT_TQ, ATT_TQ), ATT_TQ)
        lses = [ld[w, rows, :] for w in range(len(DILATIONS))]
        top = jnp.maximum(jnp.maximum(lses[0], lses[1]), lses[2])
        wts = [jnp.exp(l - top) for l in lses]
        num = wts[0] * od[0, rows, :] + wts[1] * od[1, rows, :] + wts[2] * od[2, rows, :]
        o_ref[0, rows, :] = (num / (wts[0] + wts[1] + wts[2])).astype(o_ref.dtype)
        return carry

    lax.fori_loop(0, seq // ATT_TQ, merge, 0)


def _attention(qkv_by_dilation, batch, seq):
    n_pairs = ATT_W // LANES
    blk = (1, seq, LANES)
    part = lambda j: pl.BlockSpec(blk, lambda p, b: (b, 0, j * n_pairs + p))
    out = pl.pallas_call(
        functools.partial(_att_kernel, seq=seq),
        out_shape=jax.ShapeDtypeStruct((batch, seq, ATT_W), _BF),
        grid=(n_pairs, batch),
        in_specs=[part(j) for _ in DILATIONS for j in range(3)],
        out_specs=part(0),
        scratch_shapes=[pltpu.VMEM((len(DILATIONS), seq, LANES), _F32)] * 2
        + [pltpu.VMEM((len(DILATIONS), 2, ATT_VARIANTS, ATT_TQ, ATT_TK), _F32),
           pltpu.VMEM((2, ATT_GROUP, 2, ATT_TQ, ATT_TK), _F32),
           pltpu.VMEM((2, ATT_GROUP, 2, ATT_TQ, ATT_TK), _BF),
           pltpu.VMEM((2, ATT_GROUP, ATT_TQ, LANES), _F32)],
        compiler_params=_params(2),
        name="attention",
    )(*[a for a in qkv_by_dilation for _ in range(3)])
    return out.reshape(batch * seq, ATT_W)


RET_UNROLL = 8
RET_GROUP = 2


def _ret_kernel(logit_ref, q_ref, k_ref, v_ref, g_ref, o_ref, sf, sb, lhs_buf, mix_buf, *, seq):
    head = pl.program_id(1)
    c = RET_CHUNK
    n_chunks = seq // c
    scale = RET_HEAD_DIM ** -0.5

    def log_sigmoid(x):
        return jnp.minimum(x, 0.0) - jnp.log(1.0 + jnp.exp(-jnp.abs(x)))

    lg_f = log_sigmoid(jnp.full((c, LANES), logit_ref[0, head], _F32))
    lg_b = log_sigmoid(jnp.full((c, LANES), logit_ref[1, head], _F32))
    pos = lax.broadcasted_iota(jnp.int32, (c, LANES), 0).astype(_F32)
    col = lax.broadcasted_iota(jnp.int32, (c, LANES), 1).astype(_F32)
    kw_f = jnp.exp(lg_f * (c - 1.0 - pos)) * scale
    kw_b = jnp.exp(lg_b * pos) * scale
    qw_f = jnp.exp(lg_f * (pos + 1.0))
    qw_b = jnp.exp(lg_b * (c - pos))
    g_f = jnp.exp(lg_f * c)
    g_b = jnp.exp(lg_b * c)
    rel = pos - col
    decay = jnp.where(rel >= 0, jnp.exp(lg_f * jnp.maximum(rel, 0.0)),
                      jnp.exp(lg_b * jnp.maximum(-rel, 0.0))) * scale
    lane = lax.broadcasted_iota(jnp.int32, (1, LANES), 1)
    real = lane < RET_HEAD_DIM

    def increments(it, carry):
        for u in range(RET_UNROLL):
            i = it * RET_UNROLL + u
            t0 = pl.multiple_of(i * c, c)
            k = k_ref[0, pl.ds(t0, c), :].astype(_F32)
            v = v_ref[0, pl.ds(t0, c), :]
            sf[i] = lax.dot_general((k * kw_f).astype(_BF), v, _TN, preferred_element_type=_F32)
            sb[i] = lax.dot_general((k * kw_b).astype(_BF), v, _TN, preferred_element_type=_F32)
        return carry

    lax.fori_loop(0, n_chunks // RET_UNROLL, increments, 0)

    def scan_f(i, state):
        inc = sf[i]
        sf[i] = state
        return state * g_f + inc

    def scan_b(j, state):
        i = n_chunks - 1 - j
        inc = sb[i]
        sb[i] = state
        return state * g_b + inc

    lax.fori_loop(0, n_chunks, scan_f, jnp.zeros((LANES, LANES), _F32))
    lax.fori_loop(0, n_chunks, scan_b, jnp.zeros((LANES, LANES), _F32))

    def chunk_rows(grp, u):
        i = grp * RET_GROUP + u
        return i, pl.ds(_aligned(i * c, c), c)

    def weigh(grp, slot):
        for u in range(RET_GROUP):
            _, rows = chunk_rows(grp, u)
            q = q_ref[0, rows, :]
            qf = q.astype(_F32)
            scores = lax.dot_general(q, k_ref[0, rows, :], _NT, preferred_element_type=_F32) * decay
            lhs_buf[slot, u, 0] = scores.astype(_BF)
            lhs_buf[slot, u, 1] = (qf * qw_f).astype(_BF)
            lhs_buf[slot, u, 2] = (qf * qw_b).astype(_BF)

    def mix(grp, slot):
        for u in range(RET_GROUP):
            i, rows = chunk_rows(grp, u)
            o = jnp.dot(lhs_buf[slot, u, 0], v_ref[0, rows, :], preferred_element_type=_F32)
            o = o + jnp.dot(lhs_buf[slot, u, 1], sf[i].astype(_BF), preferred_element_type=_F32)
            o = o + jnp.dot(lhs_buf[slot, u, 2], sb[i].astype(_BF), preferred_element_type=_F32)
            mix_buf[slot, u] = o

    def finish(grp, slot):
        for u in range(RET_GROUP):
            _, rows = chunk_rows(grp, u)
            o = mix_buf[slot, u]
            mu = jnp.sum(o, axis=-1, keepdims=True) * (1.0 / RET_HEAD_DIM)
            dev = jnp.where(real, o - mu, 0.0)
            var = jnp.sum(dev * dev, axis=-1, keepdims=True) * (1.0 / RET_HEAD_DIM)
            g = g_ref[0, rows, :]
            o_ref[0, rows, :] = (dev * lax.rsqrt(var + NORM_EPS) * (g / (1.0 + jnp.exp(-g)))).astype(o_ref.dtype)

    n_groups = n_chunks // RET_GROUP
    weigh(0, 0)
    mix(0, 0)
    weigh(1, 1)

    def steady(t, carry):
        for slot in range(2):
            grp = 2 * t + slot
            finish(grp, slot)
            mix(grp + 1, 1 - slot)
            weigh(grp + 2, slot)
        return carry

    lax.fori_loop(0, n_groups // 2 - 1, steady, 0)
    finish(n_groups - 2, 0)
    mix(n_groups - 1, 1)
    finish(n_groups - 1, 1)


def _retention(qkv, gate, logits, batch, seq):
    qkv = qkv.reshape(batch, seq, _IN_RET)
    gate = gate.reshape(batch, seq, RET_PAD_W)
    blk = (1, seq, LANES)
    n_chunks = seq // RET_CHUNK
    out = pl.pallas_call(
        functools.partial(_ret_kernel, seq=seq),
        out_shape=jax.ShapeDtypeStruct((batch, seq, RET_PAD_W), _BF),
        grid=(batch, RET_HEADS),
        in_specs=[pl.BlockSpec(memory_space=pltpu.SMEM),
                  pl.BlockSpec(blk, lambda b, h: (b, 0, 3 * h)),
                  pl.BlockSpec(blk, lambda b, h: (b, 0, 3 * h + 1)),
                  pl.BlockSpec(blk, lambda b, h: (b, 0, 3 * h + 2)),
                  pl.BlockSpec(blk, lambda b, h: (b, 0, h))],
        out_specs=pl.BlockSpec(blk, lambda b, h: (b, 0, h)),
        scratch_shapes=[pltpu.VMEM((n_chunks, LANES, LANES), _F32)] * 2
        + [pltpu.VMEM((2, RET_GROUP, 3, RET_CHUNK, LANES), _BF),
           pltpu.VMEM((2, RET_GROUP, RET_CHUNK, LANES), _F32)],
        compiler_params=_params(2),
        name="retention",
    )(logits, qkv, qkv, qkv, gate)
    return out.reshape(batch * seq, RET_PAD_W)


def _mix_out_ffn_kernel(att_ref, ret_ref, b_ref, u_ref, up_ref, un_ref, cw_ref, wa_ref, wr_ref, wc_ref,
                        mix_gain_ref, x_ref, ffn_gain_ref, wg_ref, wu_ref, wd_ref, o_ref, *, tiles_per_seq):
    tm = u_ref.shape[0]
    t = pl.program_id(0) % tiles_per_seq
    u = u_ref[...]
    prev_row = jnp.where(t == 0, 0.0, up_ref[7:8, :])
    next_row = jnp.where(t == tiles_per_seq - 1, 0.0, un_ref[0:1, :])
    row_id = lax.broadcasted_iota(jnp.int32, u.shape, 0)
    u_prev = jnp.where(row_id == 0, prev_row, pltpu.roll(u, 1, 0))
    u_next = jnp.where(row_id == tm - 1, next_row, pltpu.roll(u, tm - 1, 0))
    conv = cw_ref[0:1, :] * u_prev + cw_ref[1:2, :] * u + cw_ref[2:3, :] * u_next
    cv = (b_ref[...] * conv).astype(_BF)
    for rows in _subtiles(tm):
        y = jnp.dot(att_ref[rows, :], wa_ref[...], preferred_element_type=_F32)
        y = y + jnp.dot(ret_ref[rows, :], wr_ref[...], preferred_element_type=_F32)
        y = y + jnp.dot(cv[rows], wc_ref[...], preferred_element_type=_F32)
        x_mid = x_ref[rows, :] + _rms(y, mix_gain_ref[...])
        o_ref[rows, :] = _swiglu_residual(x_mid, ffn_gain_ref, wg_ref, wu_ref, wd_ref)


def _mix_out_ffn(att, ret, b, u, conv_w, wa, wr, wc, mix_gain, x, ffn_gains, wg, wu, wd, seq):
    n = x.shape[0]
    tm = FFN_SUBTILES * FFN_SUBTILE
    sub = 8
    row = lambda i: (i, 0)
    prev_blk = lambda i: (jnp.maximum(i * (tm // sub) - 1, 0), 0)
    next_blk = lambda i: (jnp.minimum((i + 1) * (tm // sub), n // sub - 1), 0)
    return pl.pallas_call(
        functools.partial(_mix_out_ffn_kernel, tiles_per_seq=seq // tm),
        out_shape=jax.ShapeDtypeStruct(x.shape, x.dtype),
        grid=(n // tm,),
        in_specs=[pl.BlockSpec((tm, ATT_W), row), pl.BlockSpec((tm, RET_PAD_W), row),
                  pl.BlockSpec((tm, CONV_W), row), pl.BlockSpec((tm, CONV_W), row),
                  pl.BlockSpec((sub, CONV_W), prev_blk), pl.BlockSpec((sub, CONV_W), next_blk),
                  _resident((3, CONV_W)), _resident((ATT_W, D_MODEL)), _resident((RET_PAD_W, D_MODEL)),
                  _resident((CONV_W, D_MODEL)), _resident((1, D_MODEL)),
                  pl.BlockSpec((tm, D_MODEL), row), _resident((2, D_MODEL)),
                  _resident((D_MODEL, D_FF)), _resident((D_MODEL, D_FF)), _resident((D_FF, D_MODEL))],
        out_specs=pl.BlockSpec((tm, D_MODEL), row),
        compiler_params=_params(1),
        name="mix_out_ffn",
    )(att, ret, b, u, u, u, conv_w, wa, wr, wc, mix_gain, x, ffn_gains, wg, wu, wd)


def _pad_heads(w, axis):
    shape = w.shape[:axis] + (RET_HEADS, RET_HEAD_DIM) + w.shape[axis + 1:]
    pad = [(0, 0)] * (len(shape))
    pad[axis + 1] = (0, LANES - RET_HEAD_DIM)
    w = jnp.pad(w.reshape(shape), pad)
    return w.reshape(w.shape[:axis] + (RET_PAD_W,) + w.shape[axis + 2:])


def _mix_in_weights(w):
    att = w[:, :_IN_ATT]
    c0 = _IN_ATT
    parts = [_pad_heads(w[:, c0 + j * RET_W:c0 + (j + 1) * RET_W], 1).reshape(D_MODEL, RET_HEADS, LANES)
             for j in range(4)]
    qkv = jnp.stack(parts[:3], axis=2).reshape(D_MODEL, _IN_RET)
    gate = parts[3].reshape(D_MODEL, _IN_GATE)
    conv = w[:, c0 + 4 * RET_W:]
    return jnp.concatenate([att, qkv, gate, conv], axis=1).astype(_BF)


def _trunk(x, batch, seq, weights):
    for layer in weights:
        x = _ffn(x, layer["gain"][0:2], *layer["ffn1"])
        att_qkv, ret_qkv, gate, b, u = _inproj(x, layer["gain"][2:3], layer["w_in"], batch, seq)
        att = _attention(att_qkv, batch, seq)
        ret = _retention(ret_qkv, gate, layer["logit"], batch, seq)
        x = _mix_out_ffn(att, ret, b, u, layer["conv_w"], *layer["w_out"], layer["gain"][3:4], x,
                         layer["gain"][4:6], *layer["ffn2"], seq)
    return x


def kernel(x_prompt, x_sample, norm_gain, ffn1_w_gate, ffn1_w_up, ffn1_w_down, w_mix_in, conv_w,
           ret_decay_logit, w_mix_out, ffn2_w_gate, ffn2_w_up, ffn2_w_down):
    weights = []
    for l in range(DEPTH):
        wo = w_mix_out[l]
        weights.append(dict(
            gain=norm_gain[l],
            ffn1=(ffn1_w_gate[l].astype(_BF), ffn1_w_up[l].astype(_BF), ffn1_w_down[l].astype(_BF)),
            ffn2=(ffn2_w_gate[l].astype(_BF), ffn2_w_up[l].astype(_BF), ffn2_w_down[l].astype(_BF)),
            w_in=_mix_in_weights(w_mix_in[l]),
            conv_w=conv_w[l],
            logit=ret_decay_logit[l],
            w_out=(wo[:ATT_W].astype(_BF), _pad_heads(wo[ATT_W:ATT_W + RET_W], 0).astype(_BF),
                   wo[ATT_W + RET_W:].astype(_BF)),
        ))
    outs = []
    for x in (x_prompt, x_sample):
        batch, seq, _ = x.shape
        outs.append(_trunk(x.reshape(batch * seq, D_MODEL), batch, seq, weights).reshape(x.shape))
    return tuple(outs)
```

```python
import functools
import math

import jax
import jax.numpy as jnp
import numpy as np
from jax import lax
from jax.experimental import pallas as pl
from jax.experimental.pallas import tpu as pltpu

D_MODEL = 1024
D_FF = 2816
DEPTH = 2
HEAD_DIM = 64
ATT_HEADS = 6
ATT_W = ATT_HEADS * HEAD_DIM
DILATIONS = (1, 4, 16)
ATT_HALF = 64
RET_HEADS = 4
RET_HEAD_DIM = 96
RET_W = RET_HEADS * RET_HEAD_DIM
RET_CHUNK = 128
CONV_W = 256
NORM_EPS = 1e-6
NEG_INF = -1e30

LANES = 128
RET_PAD_W = RET_HEADS * LANES
ATT_TQ = 128
ATT_TK = ATT_TQ + 2 * ATT_HALF
FF_CHUNK = 256
TOKEN_TILE = 512
FFN_SUBTILE = 512
FFN_SUBTILES = 2
VMEM_LIMIT = 56 * 1024 * 1024

_BF = jnp.bfloat16
_F32 = jnp.float32
_NT = (((1,), (1,)), ((), ()))
_TN = (((0,), (0,)), ((), ()))


def _alibi_slope_list(n):
    def pow2(m):
        start = 2.0 ** (-8.0 / m)
        return [start ** (i + 1) for i in range(m)]
    if math.log2(n).is_integer():
        return pow2(n)
    c = 2 ** math.floor(math.log2(n))
    return pow2(c) + _alibi_slope_list(2 * c)[0::2][: n - c]


_SLOPES = [float(np.float32(s)) for s in _alibi_slope_list(ATT_HEADS)]


def _aligned(x, m):
    return x if isinstance(x, int) else pl.multiple_of(x, m)


def _rms(x, g):
    return x * lax.rsqrt(jnp.mean(x * x, axis=-1, keepdims=True) + NORM_EPS) * g


def _params(n_axes):
    return pltpu.CompilerParams(dimension_semantics=("arbitrary",) * n_axes,
                                vmem_limit_bytes=VMEM_LIMIT)


def _resident(shape):
    return pl.BlockSpec(shape, lambda *_: (0,) * len(shape), pipeline_mode=pl.Buffered(1))


def _swiglu_residual(x, gain_ref, wg_ref, wu_ref, wd_ref):
    h = _rms(x, gain_ref[0:1, :]).astype(_BF)
    acc = jnp.zeros(x.shape, _F32)
    for c in range(D_FF // FF_CHUNK):
        sl = slice(c * FF_CHUNK, (c + 1) * FF_CHUNK)
        g = jnp.dot(h, wg_ref[:, sl], preferred_element_type=_F32)
        u = jnp.dot(h, wu_ref[:, sl], preferred_element_type=_F32)
        a = (g / (1.0 + jnp.exp(-g)) * u).astype(_BF)
        acc = acc + jnp.dot(a, wd_ref[sl, :], preferred_element_type=_F32)
    return x + 0.5 * _rms(acc, gain_ref[1:2, :])


def _subtiles(n_rows):
    return [slice(r, r + FFN_SUBTILE) for r in range(0, n_rows, FFN_SUBTILE)]


def _ffn_kernel(x_ref, gain_ref, wg_ref, wu_ref, wd_ref, o_ref):
    for rows in _subtiles(x_ref.shape[0]):
        o_ref[rows, :] = _swiglu_residual(x_ref[rows, :], gain_ref, wg_ref, wu_ref, wd_ref)


def _ffn(x, gains, wg, wu, wd):
    n = x.shape[0]
    tm = FFN_SUBTILES * FFN_SUBTILE
    row = lambda i: (i, 0)
    return pl.pallas_call(
        _ffn_kernel,
        out_shape=jax.ShapeDtypeStruct(x.shape, x.dtype),
        grid=(n // tm,),
        in_specs=[pl.BlockSpec((tm, D_MODEL), row), _resident((2, D_MODEL)),
                  _resident((D_MODEL, D_FF)), _resident((D_MODEL, D_FF)), _resident((D_FF, D_MODEL))],
        out_specs=pl.BlockSpec((tm, D_MODEL), row),
        compiler_params=_params(1),
        name="ffn",
    )(x, gains, wg, wu, wd)


_IN_ATT = 3 * ATT_W
_IN_RET = RET_HEADS * 3 * LANES
_IN_GATE = RET_PAD_W
_IN_CONV = 3 * CONV_W
_IN_TOTAL = _IN_ATT + _IN_RET + _IN_GATE + _IN_CONV
IN_CHUNK = 256


def _inproj_kernel(x_ref, gain_ref, w_ref, att1_ref, att4_ref, att16_ref, ret_ref, gate_ref, b_ref, u_ref,
                   stage, stage4):
    tm = x_ref.shape[0]
    h = _rms(x_ref[...], gain_ref[...]).astype(_BF)
    c0 = 0
    def project(start, width):
        for c in range(0, width, IN_CHUNK):
            cw = min(IN_CHUNK, width - c)
            yield c, jnp.dot(h, w_ref[:, start + c:start + c + cw], preferred_element_type=_F32)

    n_slabs = _IN_ATT // LANES
    for c, z in project(c0, _IN_ATT):
        for s in range(c // LANES, (c + z.shape[1]) // LANES):
            zs = z[:, s * LANES - c:(s + 1) * LANES - c]
            if (s + 1) * LANES <= ATT_W:
                zs = zs * (HEAD_DIM ** -0.5)
            stage[s] = zs
            att1_ref[0, s] = zs.astype(_BF)
    d4, d16 = DILATIONS[1], DILATIONS[2]
    rows4, rows16 = tm // d4, tm // d16
    for s in range(n_slabs):
        for r4 in range(d4):
            x = stage[s, pl.ds(r4, rows4, stride=d4), :]
            att4_ref[0, s, r4] = x.astype(_BF)
            stage4[s, r4 * rows4:(r4 + 1) * rows4, :] = x
        for r4 in range(d4):
            for j in range(d16 // d4):
                x = stage4[s, pl.ds(r4 * rows4 + j, rows16, stride=d16 // d4), :]
                att16_ref[0, s, r4 + d4 * j] = x.astype(_BF)
    c0 += _IN_ATT

    def slabs(ref, c, z, dtype):
        for s in range(z.shape[1] // LANES):
            ref[0, c // LANES + s] = z[:, s * LANES:(s + 1) * LANES].astype(dtype)

    for c, z in project(c0, _IN_RET):
        slabs(ret_ref, c, z, _BF)
    c0 += _IN_RET
    for c, z in project(c0, _IN_GATE):
        slabs(gate_ref, c, z, _F32)
    c0 += _IN_GATE
    (_, zb), (_, zc), (_, zu) = project(c0, _IN_CONV)
    b_ref[...] = zb
    u_ref[...] = zc * zu


def _inproj(x, gain, w, batch, seq):
    n = x.shape[0]
    tm = TOKEN_TILE
    tiles_per_seq = seq // tm
    row = lambda i: (i, 0)
    where = lambda i: (i // tiles_per_seq, i % tiles_per_seq)

    def slab_out(width, dtype, d=1):
        n_slabs = width // LANES
        if d == 1:
            return (jax.ShapeDtypeStruct((batch, n_slabs, seq, LANES), dtype),
                    pl.BlockSpec((1, n_slabs, tm, LANES), lambda i: (where(i)[0], 0, where(i)[1], 0)))
        return (jax.ShapeDtypeStruct((batch, n_slabs, d, seq // d, LANES), dtype),
                pl.BlockSpec((1, n_slabs, d, tm // d, LANES), lambda i: (where(i)[0], 0, 0, where(i)[1], 0)))

    flat = [(jax.ShapeDtypeStruct((n, CONV_W), _F32), pl.BlockSpec((tm, CONV_W), row))] * 2
    outs = [slab_out(_IN_ATT, _BF, d) for d in DILATIONS] + [slab_out(_IN_RET, _BF), slab_out(_IN_GATE, _F32)] + flat
    att1, att4, att16, ret, gate, b, u = pl.pallas_call(
        _inproj_kernel,
        out_shape=[o[0] for o in outs],
        grid=(n // tm,),
        in_specs=[pl.BlockSpec((tm, D_MODEL), row), _resident((1, D_MODEL)),
                  _resident((D_MODEL, _IN_TOTAL))],
        out_specs=[o[1] for o in outs],
        scratch_shapes=[pltpu.VMEM((_IN_ATT // LANES, tm, LANES), _F32)] * 2,
        compiler_params=_params(1),
        name="inproj",
    )(x, gain, w)
    att = [a.reshape(batch, _IN_ATT // LANES, seq, LANES) for a in (att1, att4, att16)]
    return att, ret, gate, b, u


ATT_GROUP = 2
ATT_VARIANTS = 6


def _att_kernel(*refs, seq):
    n_win = len(DILATIONS)
    qkv_refs = [refs[3 * w:3 * w + 3] for w in range(n_win)]
    o_ref, od, ld, bias_ref, s_buf, p_buf, m_buf = refs[3 * n_win:]
    pair = pl.program_id(0)
    lane = lax.broadcasted_iota(jnp.int32, (1, LANES), 1)
    head_a = lane < HEAD_DIM
    zero_bf = jnp.zeros((), _BF)

    mask_a = jnp.broadcast_to(jnp.where(head_a, 1.0, 0.0), (ATT_TK, LANES)).astype(_BF)
    mask_b = jnp.broadcast_to(jnp.where(head_a, 0.0, 1.0), (ATT_TK, LANES)).astype(_BF)

    def heads(x):
        return jnp.where(head_a, x, zero_bf), jnp.where(head_a, zero_bf, x)

    def slope(j):
        s = jnp.float32(_SLOPES[j])
        for p in range(1, ATT_HEADS // 2):
            s = jnp.where(pair == p, jnp.float32(_SLOPES[2 * p + j]), s)
        return s

    @pl.when(pl.program_id(1) == 0)
    def _():
        qi = lax.broadcasted_iota(jnp.int32, (ATT_TQ, ATT_TK), 0)
        kc = lax.broadcasted_iota(jnp.int32, (ATT_TQ, ATT_TK), 1)
        for w, d in enumerate(DILATIONS):
            single = seq // d == ATT_TQ
            variants = {3: (ATT_HALF, True, True)} if single else {
                0: (ATT_HALF, False, False), 1: (ATT_HALF, True, False), 2: (ATT_HALF, False, True)}
            variants[4] = (0, True, single)
            variants[5] = (2 * ATT_HALF, single, True)
            for variant, (shift, first, last) in variants.items():
                rel = jnp.abs(qi + shift - kc)
                valid = rel <= ATT_HALF
                if first:
                    valid = valid & (kc >= shift)
                if last:
                    valid = valid & (kc < shift + ATT_TQ)
                dist = (d * rel).astype(_F32)
                for j in range(2):
                    bias_ref[w, j, variant] = jnp.where(valid, -slope(j) * dist, NEG_INF)

    for w, d in enumerate(DILATIONS):
        q_ref, k_ref, v_ref = qkv_refs[w]
        sub_len = seq // d
        n_blocks = sub_len // ATT_TQ
        n_total = d * n_blocks

        def coords(blk):
            r = blk // n_blocks
            b = blk - r * n_blocks
            if isinstance(blk, int):
                at_start, at_end = int(blk == 0), int(blk == n_total - 1)
                variant = 4 if at_start else 5 if at_end else int(b == 0) + 2 * int(b == n_blocks - 1)
            else:
                at_start, at_end = jnp.where(blk == 0, 1, 0), jnp.where(blk == n_total - 1, 1, 0)
                edge = jnp.where(b == 0, 1, 0) + jnp.where(b == n_blocks - 1, 2, 0)
                variant = jnp.where(blk == 0, 4, jnp.where(blk == n_total - 1, 5, edge))
            q0 = _aligned(blk * ATT_TQ, ATT_TQ)
            t0 = _aligned(blk * ATT_TQ - ATT_HALF + ATT_HALF * (at_start - at_end), ATT_HALF)
            return r, b, q0, t0, variant

        def score(grp, slot):
            for u in range(ATT_GROUP):
                _, _, q0, t0, variant = coords(grp * ATT_GROUP + u)
                q2 = jnp.concatenate(heads(q_ref[0, 0,pl.ds(q0, ATT_TQ), :]), axis=0)
                s = lax.dot_general(q2, k_ref[0, 0,pl.ds(t0, ATT_TK), :], _NT, preferred_element_type=_F32)
                for j in range(2):
                    s_buf[slot, u, j] = s[j * ATT_TQ:(j + 1) * ATT_TQ] + bias_ref[w, j, variant]

        def softmax(slot):
            for u in range(ATT_GROUP):
                tops = [jnp.max(s_buf[slot, u, j], axis=-1, keepdims=True) for j in range(2)]
                m_buf[slot, u] = jnp.where(head_a, tops[0], tops[1])
                for j in range(2):
                    p_buf[slot, u, j] = jnp.exp(s_buf[slot, u, j] - tops[j]).astype(_BF)

        def output(grp, slot):
            for u in range(ATT_GROUP):
                r, b, q0, t0, _ = coords(grp * ATT_GROUP + u)
                v_a, v_b = heads(v_ref[0, 0,pl.ds(t0, ATT_TK), :])
                acc = jnp.dot(p_buf[slot, u, 0], jnp.concatenate([v_a, mask_a], axis=1),
                              preferred_element_type=_F32)
                acc = acc + jnp.dot(p_buf[slot, u, 1], jnp.concatenate([v_b, mask_b], axis=1),
                                    preferred_element_type=_F32)
                den = acc[:, LANES:]
                dst = pl.ds(r + d * b * ATT_TQ, ATT_TQ, stride=d) if d > 1 else pl.ds(q0, ATT_TQ)
                od[w, dst, :] = acc[:, :LANES] / den
                ld[w, dst, :] = m_buf[slot, u] + jnp.log(den)

        n_groups = d * n_blocks // ATT_GROUP

        score(0, 0)
        softmax(0)
        score(1, 1)

        def steady(t, carry):
            for slot in range(2):
                g = 2 * t + slot
                output(g, slot)
                softmax(1 - slot)
                score(g + 2, slot)
            return carry

        lax.fori_loop(0, n_groups // 2 - 1, steady, 0)
        output(n_groups - 2, 0)
        softmax(1)
        output(n_groups - 1, 1)

    def merge(i, carry):
        rows = pl.ds(pl.multiple_of(i * ATT_TQ, ATT_TQ), ATT_TQ)
        lses = [ld[w, rows, :] for w in range(len(DILATIONS))]
        top = jnp.maximum(jnp.maximum(lses[0], lses[1]), lses[2])
        wts = [jnp.exp(l - top) for l in lses]
        num = wts[0] * od[0, rows, :] + wts[1] * od[1, rows, :] + wts[2] * od[2, rows, :]
        o_ref[0, 0, rows, :] = (num / (wts[0] + wts[1] + wts[2])).astype(o_ref.dtype)
        return carry

    lax.fori_loop(0, seq // ATT_TQ, merge, 0)


def _attention(qkv_by_dilation, batch, seq):
    n_pairs = ATT_W // LANES
    blk = (1, 1, seq, LANES)
    part = lambda j: pl.BlockSpec(blk, lambda p, b: (b, j * n_pairs + p, 0, 0))
    return pl.pallas_call(
        functools.partial(_att_kernel, seq=seq),
        out_shape=jax.ShapeDtypeStruct((batch, n_pairs, seq, LANES), _BF),
        grid=(n_pairs, batch),
        in_specs=[part(j) for _ in DILATIONS for j in range(3)],
        out_specs=part(0),
        scratch_shapes=[pltpu.VMEM((len(DILATIONS), seq, LANES), _F32)] * 2
        + [pltpu.VMEM((len(DILATIONS), 2, ATT_VARIANTS, ATT_TQ, ATT_TK), _F32),
           pltpu.VMEM((2, ATT_GROUP, 2, ATT_TQ, ATT_TK), _F32),
           pltpu.VMEM((2, ATT_GROUP, 2, ATT_TQ, ATT_TK), _BF),
           pltpu.VMEM((2, ATT_GROUP, ATT_TQ, LANES), _F32)],
        compiler_params=_params(2),
        name="attention",
    )(*[a for a in qkv_by_dilation for _ in range(3)])


RET_UNROLL = 8
RET_GROUP = 2


def _ret_kernel(logit_ref, q_ref, k_ref, v_ref, g_ref, o_ref, sf, sb, lhs_buf, mix_buf, *, seq):
    head = pl.program_id(1)
    c = RET_CHUNK
    n_chunks = seq // c
    scale = RET_HEAD_DIM ** -0.5

    def log_sigmoid(x):
        return jnp.minimum(x, 0.0) - jnp.log(1.0 + jnp.exp(-jnp.abs(x)))

    lg_f = log_sigmoid(jnp.full((c, LANES), logit_ref[0, head], _F32))
    lg_b = log_sigmoid(jnp.full((c, LANES), logit_ref[1, head], _F32))
    pos = lax.broadcasted_iota(jnp.int32, (c, LANES), 0).astype(_F32)
    col = lax.broadcasted_iota(jnp.int32, (c, LANES), 1).astype(_F32)
    kw_f = jnp.exp(lg_f * (c - 1.0 - pos)) * scale
    kw_b = jnp.exp(lg_b * pos) * scale
    qw_f = jnp.exp(lg_f * (pos + 1.0))
    qw_b = jnp.exp(lg_b * (c - pos))
    g_f = jnp.exp(lg_f * c)
    g_b = jnp.exp(lg_b * c)
    rel = pos - col
    decay = jnp.where(rel >= 0, jnp.exp(lg_f * jnp.maximum(rel, 0.0)),
                      jnp.exp(lg_b * jnp.maximum(-rel, 0.0))) * scale
    lane = lax.broadcasted_iota(jnp.int32, (1, LANES), 1)
    real = lane < RET_HEAD_DIM

    def increments(it, carry):
        for u in range(RET_UNROLL):
            i = it * RET_UNROLL + u
            t0 = pl.multiple_of(i * c, c)
            k = k_ref[0, 0,pl.ds(t0, c), :].astype(_F32)
            v = v_ref[0, 0,pl.ds(t0, c), :]
            sf[i] = lax.dot_general((k * kw_f).astype(_BF), v, _TN, preferred_element_type=_F32)
            sb[i] = lax.dot_general((k * kw_b).astype(_BF), v, _TN, preferred_element_type=_F32)
        return carry

    lax.fori_loop(0, n_chunks // RET_UNROLL, increments, 0)

    def scan_f(i, state):
        inc = sf[i]
        sf[i] = state
        return state * g_f + inc

    def scan_b(j, state):
        i = n_chunks - 1 - j
        inc = sb[i]
        sb[i] = state
        return state * g_b + inc

    lax.fori_loop(0, n_chunks, scan_f, jnp.zeros((LANES, LANES), _F32))
    lax.fori_loop(0, n_chunks, scan_b, jnp.zeros((LANES, LANES), _F32))

    def chunk_rows(grp, u):
        i = grp * RET_GROUP + u
        return i, pl.ds(_aligned(i * c, c), c)

    def weigh(grp, slot):
        for u in range(RET_GROUP):
            _, rows = chunk_rows(grp, u)
            q = q_ref[0, 0,rows, :]
            qf = q.astype(_F32)
            scores = lax.dot_general(q, k_ref[0, 0,rows, :], _NT, preferred_element_type=_F32) * decay
            lhs_buf[slot, u, 0] = scores.astype(_BF)
            lhs_buf[slot, u, 1] = (qf * qw_f).astype(_BF)
            lhs_buf[slot, u, 2] = (qf * qw_b).astype(_BF)

    def mix(grp, slot):
        for u in range(RET_GROUP):
            i, rows = chunk_rows(grp, u)
            o = jnp.dot(lhs_buf[slot, u, 0], v_ref[0, 0,rows, :], preferred_element_type=_F32)
            o = o + jnp.dot(lhs_buf[slot, u, 1], sf[i].astype(_BF), preferred_element_type=_F32)
            o = o + jnp.dot(lhs_buf[slot, u, 2], sb[i].astype(_BF), preferred_element_type=_F32)
            mix_buf[slot, u] = o

    def finish(grp, slot):
        for u in range(RET_GROUP):
            _, rows = chunk_rows(grp, u)
            o = mix_buf[slot, u]
            mu = jnp.sum(o, axis=-1, keepdims=True) * (1.0 / RET_HEAD_DIM)
            dev = jnp.where(real, o - mu, 0.0)
            var = jnp.sum(dev * dev, axis=-1, keepdims=True) * (1.0 / RET_HEAD_DIM)
            g = g_ref[0, 0, rows, :]
            o_ref[0, 0, rows, :] = (dev * lax.rsqrt(var + NORM_EPS) * (g / (1.0 + jnp.exp(-g)))).astype(o_ref.dtype)

    n_groups = n_chunks // RET_GROUP
    weigh(0, 0)
    mix(0, 0)
    weigh(1, 1)

    def steady(t, carry):
        for slot in range(2):
            grp = 2 * t + slot
            finish(grp, slot)
            mix(grp + 1, 1 - slot)
            weigh(grp + 2, slot)
        return carry

    lax.fori_loop(0, n_groups // 2 - 1, steady, 0)
    finish(n_groups - 2, 0)
    mix(n_groups - 1, 1)
    finish(n_groups - 1, 1)


def _retention(qkv, gate, logits, batch, seq):
    blk = (1, 1, seq, LANES)
    n_chunks = seq // RET_CHUNK
    return pl.pallas_call(
        functools.partial(_ret_kernel, seq=seq),
        out_shape=jax.ShapeDtypeStruct((batch, RET_HEADS, seq, LANES), _BF),
        grid=(batch, RET_HEADS),
        in_specs=[pl.BlockSpec(memory_space=pltpu.SMEM),
                  pl.BlockSpec(blk, lambda b, h: (b, 3 * h, 0, 0)),
                  pl.BlockSpec(blk, lambda b, h: (b, 3 * h + 1, 0, 0)),
                  pl.BlockSpec(blk, lambda b, h: (b, 3 * h + 2, 0, 0)),
                  pl.BlockSpec(blk, lambda b, h: (b, h, 0, 0))],
        out_specs=pl.BlockSpec(blk, lambda b, h: (b, h, 0, 0)),
        scratch_shapes=[pltpu.VMEM((n_chunks, LANES, LANES), _F32)] * 2
        + [pltpu.VMEM((2, RET_GROUP, 3, RET_CHUNK, LANES), _BF),
           pltpu.VMEM((2, RET_GROUP, RET_CHUNK, LANES), _F32)],
        compiler_params=_params(2),
        name="retention",
    )(logits, qkv, qkv, qkv, gate)


def _mix_out_ffn_kernel(att_ref, ret_ref, b_ref, u_ref, up_ref, un_ref, cw_ref, wa_ref, wr_ref, wc_ref,
                        mix_gain_ref, x_ref, ffn_gain_ref, wg_ref, wu_ref, wd_ref, o_ref, *, tiles_per_seq):
    tm = u_ref.shape[0]
    t = pl.program_id(0) % tiles_per_seq
    u = u_ref[...]
    prev_row = jnp.where(t == 0, 0.0, up_ref[7:8, :])
    next_row = jnp.where(t == tiles_per_seq - 1, 0.0, un_ref[0:1, :])
    row_id = lax.broadcasted_iota(jnp.int32, u.shape, 0)
    u_prev = jnp.where(row_id == 0, prev_row, pltpu.roll(u, 1, 0))
    u_next = jnp.where(row_id == tm - 1, next_row, pltpu.roll(u, tm - 1, 0))
    conv = cw_ref[0:1, :] * u_prev + cw_ref[1:2, :] * u + cw_ref[2:3, :] * u_next
    cv = (b_ref[...] * conv).astype(_BF)
    def wide(ref, rows):
        return jnp.concatenate([ref[0, s, rows, :] for s in range(ref.shape[1])], axis=1)

    for rows in _subtiles(tm):
        y = jnp.dot(wide(att_ref, rows), wa_ref[...], preferred_element_type=_F32)
        y = y + jnp.dot(wide(ret_ref, rows), wr_ref[...], preferred_element_type=_F32)
        y = y + jnp.dot(cv[rows], wc_ref[...], preferred_element_type=_F32)
        x_mid = x_ref[rows, :] + _rms(y, mix_gain_ref[...])
        o_ref[rows, :] = _swiglu_residual(x_mid, ffn_gain_ref, wg_ref, wu_ref, wd_ref)


def _mix_out_ffn(att, ret, b, u, conv_w, wa, wr, wc, mix_gain, x, ffn_gains, wg, wu, wd, seq):
    n = x.shape[0]
    tm = FFN_SUBTILES * FFN_SUBTILE
    sub = 8
    row = lambda i: (i, 0)
    prev_blk = lambda i: (jnp.maximum(i * (tm // sub) - 1, 0), 0)
    next_blk = lambda i: (jnp.minimum((i + 1) * (tm // sub), n // sub - 1), 0)
    tiles_per_seq = seq // tm
    slab_rows = lambda i: (i // tiles_per_seq, 0, i % tiles_per_seq, 0)
    return pl.pallas_call(
        functools.partial(_mix_out_ffn_kernel, tiles_per_seq=tiles_per_seq),
        out_shape=jax.ShapeDtypeStruct(x.shape, x.dtype),
        grid=(n // tm,),
        in_specs=[pl.BlockSpec((1, ATT_W // LANES, tm, LANES), slab_rows),
                  pl.BlockSpec((1, RET_HEADS, tm, LANES), slab_rows),
                  pl.BlockSpec((tm, CONV_W), row), pl.BlockSpec((tm, CONV_W), row),
                  pl.BlockSpec((sub, CONV_W), prev_blk), pl.BlockSpec((sub, CONV_W), next_blk),
                  _resident((3, CONV_W)), _resident((ATT_W, D_MODEL)), _resident((RET_PAD_W, D_MODEL)),
                  _resident((CONV_W, D_MODEL)), _resident((1, D_MODEL)),
                  pl.BlockSpec((tm, D_MODEL), row), _resident((2, D_MODEL)),
                  _resident((D_MODEL, D_FF)), _resident((D_MODEL, D_FF)), _resident((D_FF, D_MODEL))],
        out_specs=pl.BlockSpec((tm, D_MODEL), row),
        compiler_params=_params(1),
        name="mix_out_ffn",
    )(att, ret, b, u, u, u, conv_w, wa, wr, wc, mix_gain, x, ffn_gains, wg, wu, wd)


def _pad_heads(w, axis):
    shape = w.shape[:axis] + (RET_HEADS, RET_HEAD_DIM) + w.shape[axis + 1:]
    pad = [(0, 0)] * (len(shape))
    pad[axis + 1] = (0, LANES - RET_HEAD_DIM)
    w = jnp.pad(w.reshape(shape), pad)
    return w.reshape(w.shape[:axis] + (RET_PAD_W,) + w.shape[axis + 2:])


def _mix_in_weights(w):
    att = w[:, :_IN_ATT]
    c0 = _IN_ATT
    parts = [_pad_heads(w[:, c0 + j * RET_W:c0 + (j + 1) * RET_W], 1).reshape(D_MODEL, RET_HEADS, LANES)
             for j in range(4)]
    qkv = jnp.stack(parts[:3], axis=2).reshape(D_MODEL, _IN_RET)
    gate = parts[3].reshape(D_MODEL, _IN_GATE)
    conv = w[:, c0 + 4 * RET_W:]
    return jnp.concatenate([att, qkv, gate, conv], axis=1).astype(_BF)


def _trunk(x, batch, seq, weights):
    for layer in weights:
        x = _ffn(x, layer["gain"][0:2], *layer["ffn1"])
        att_qkv, ret_qkv, gate, b, u = _inproj(x, layer["gain"][2:3], layer["w_in"], batch, seq)
        att = _attention(att_qkv, batch, seq)
        ret = _retention(ret_qkv, gate, layer["logit"], batch, seq)
        x = _mix_out_ffn(att, ret, b, u, layer["conv_w"], *layer["w_out"], layer["gain"][3:4], x,
                         layer["gain"][4:6], *layer["ffn2"], seq)
    return x


def kernel(x_prompt, x_sample, norm_gain, ffn1_w_gate, ffn1_w_up, ffn1_w_down, w_mix_in, conv_w,
           ret_decay_logit, w_mix_out, ffn2_w_gate, ffn2_w_up, ffn2_w_down):
    weights = []
    for l in range(DEPTH):
        wo = w_mix_out[l]
        weights.append(dict(
            gain=norm_gain[l],
            ffn1=(ffn1_w_gate[l].astype(_BF), ffn1_w_up[l].astype(_BF), ffn1_w_down[l].astype(_BF)),
            ffn2=(ffn2_w_gate[l].astype(_BF), ffn2_w_up[l].astype(_BF), ffn2_w_down[l].astype(_BF)),
            w_in=_mix_in_weights(w_mix_in[l]),
            conv_w=conv_w[l],
            logit=ret_decay_logit[l],
            w_out=(wo[:ATT_W].astype(_BF), _pad_heads(wo[ATT_W:ATT_W + RET_W], 0).astype(_BF),
                   wo[ATT_W + RET_W:].astype(_BF)),
        ))
    outs = []
    for x in (x_prompt, x_sample):
        batch, seq, _ = x.shape
        outs.append(_trunk(x.reshape(batch * seq, D_MODEL), batch, seq, weights).reshape(x.shape))
    return tuple(outs)
```

```python
import functools
import math

import jax
import jax.numpy as jnp
import numpy as np
from jax import lax
from jax.experimental import pallas as pl
from jax.experimental.pallas import tpu as pltpu

D_MODEL = 1024
D_FF = 2816
DEPTH = 2
HEAD_DIM = 64
ATT_HEADS = 6
ATT_W = ATT_HEADS * HEAD_DIM
DILATIONS = (1, 4, 16)
ATT_HALF = 64
RET_HEADS = 4
RET_HEAD_DIM = 96
RET_W = RET_HEADS * RET_HEAD_DIM
RET_CHUNK = 128
CONV_W = 256
NORM_EPS = 1e-6
NEG_INF = -1e30
LOG2E = math.log2(math.e)
ATT_Q_SCALE = HEAD_DIM ** -0.5 * LOG2E

LANES = 128
RET_PAD_W = RET_HEADS * LANES
ATT_TQ = 128
ATT_TK = ATT_TQ + 2 * ATT_HALF
FF_CHUNK = 256
TOKEN_TILE = 512
CAST_ROWS = 256
FFN_SUBTILE = 512
FFN_SUBTILES = 2
VMEM_LIMIT = 56 * 1024 * 1024

_BF = jnp.bfloat16
_F32 = jnp.float32
_NT = (((1,), (1,)), ((), ()))
_TN = (((0,), (0,)), ((), ()))


def _alibi_slope_list(n):
    def pow2(m):
        start = 2.0 ** (-8.0 / m)
        return [start ** (i + 1) for i in range(m)]
    if math.log2(n).is_integer():
        return pow2(n)
    c = 2 ** math.floor(math.log2(n))
    return pow2(c) + _alibi_slope_list(2 * c)[0::2][: n - c]


_SLOPES = [float(np.float32(s)) for s in _alibi_slope_list(ATT_HEADS)]


def _aligned(x, m):
    return x if isinstance(x, int) else pl.multiple_of(x, m)


def _rms(x, g):
    return x * lax.rsqrt(jnp.mean(x * x, axis=-1, keepdims=True) + NORM_EPS) * g


def _params(n_axes):
    return pltpu.CompilerParams(dimension_semantics=("arbitrary",) * n_axes,
                                vmem_limit_bytes=VMEM_LIMIT)


def _resident(shape):
    return pl.BlockSpec(shape, lambda *_: (0,) * len(shape), pipeline_mode=pl.Buffered(1))


def _swiglu_residual(x, gain_ref, wg_ref, wu_ref, wd_ref):
    h = _rms(x, gain_ref[0:1, :]).astype(_BF)
    acc = jnp.zeros(x.shape, _F32)
    for c in range(D_FF // FF_CHUNK):
        sl = slice(c * FF_CHUNK, (c + 1) * FF_CHUNK)
        g = jnp.dot(h, wg_ref[:, sl], preferred_element_type=_F32)
        u = jnp.dot(h, wu_ref[:, sl], preferred_element_type=_F32)
        a = (g / (1.0 + jnp.exp(-g)) * u).astype(_BF)
        acc = acc + jnp.dot(a, wd_ref[sl, :], preferred_element_type=_F32)
    return x + 0.5 * _rms(acc, gain_ref[1:2, :])


def _subtiles(n_rows):
    return [slice(r, r + FFN_SUBTILE) for r in range(0, n_rows, FFN_SUBTILE)]


def _ffn_kernel(x_ref, gain_ref, wg_ref, wu_ref, wd_ref, o_ref):
    for rows in _subtiles(x_ref.shape[0]):
        o_ref[rows, :] = _swiglu_residual(x_ref[rows, :], gain_ref, wg_ref, wu_ref, wd_ref)


def _ffn(x, gains, wg, wu, wd):
    n = x.shape[0]
    tm = FFN_SUBTILES * FFN_SUBTILE
    row = lambda i: (i, 0)
    return pl.pallas_call(
        _ffn_kernel,
        out_shape=jax.ShapeDtypeStruct(x.shape, x.dtype),
        grid=(n // tm,),
        in_specs=[pl.BlockSpec((tm, D_MODEL), row), _resident((2, D_MODEL)),
                  _resident((D_MODEL, D_FF)), _resident((D_MODEL, D_FF)), _resident((D_FF, D_MODEL))],
        out_specs=pl.BlockSpec((tm, D_MODEL), row),
        compiler_params=_params(1),
        name="ffn",
    )(x, gains, wg, wu, wd)


_IN_ATT = 3 * ATT_W
_IN_RET = RET_HEADS * 3 * LANES
_IN_GATE = RET_PAD_W
_IN_CONV = 3 * CONV_W
_IN_TOTAL = _IN_ATT + _IN_RET + _IN_GATE + _IN_CONV
IN_CHUNK = 256


def _inproj_kernel(x_ref, gain_ref, w_ref, att1_ref, att4_ref, att16_ref, ret_ref, gate_ref, b_ref, u_ref,
                   stage, stage4):
    tm = x_ref.shape[0]
    h = _rms(x_ref[...], gain_ref[...]).astype(_BF)
    c0 = 0
    def project(start, width):
        for c in range(0, width, IN_CHUNK):
            cw = min(IN_CHUNK, width - c)
            yield c, jnp.dot(h, w_ref[:, start + c:start + c + cw], preferred_element_type=_F32)

    n_slabs = _IN_ATT // LANES
    for c, z in project(c0, _IN_ATT):
        for s in range(c // LANES, (c + z.shape[1]) // LANES):
            zs = z[:, s * LANES - c:(s + 1) * LANES - c]
            if (s + 1) * LANES <= ATT_W:
                zs = zs * ATT_Q_SCALE
            stage[s] = zs
            att1_ref[0, s] = zs.astype(_BF)
    d4, d16 = DILATIONS[1], DILATIONS[2]
    rows4, rows16 = tm // d4, tm // d16
    for s in range(n_slabs):
        for r4 in range(d4):
            x = stage[s, pl.ds(r4, rows4, stride=d4), :]
            att4_ref[0, s, r4] = x.astype(_BF)
            stage4[s, r4 * rows4:(r4 + 1) * rows4, :] = x
        for r4 in range(d4):
            for j in range(d16 // d4):
                x = stage4[s, pl.ds(r4 * rows4 + j, rows16, stride=d16 // d4), :]
                att16_ref[0, s, r4 + d4 * j] = x.astype(_BF)
    c0 += _IN_ATT

    def slabs(ref, c, z, dtype):
        for s in range(z.shape[1] // LANES):
            ref[0, c // LANES + s] = z[:, s * LANES:(s + 1) * LANES].astype(dtype)

    for c, z in project(c0, _IN_RET):
        slabs(ret_ref, c, z, _BF)
    c0 += _IN_RET
    for c, z in project(c0, _IN_GATE):
        slabs(gate_ref, c, z, _F32)
    c0 += _IN_GATE
    (_, zb), (_, zc), (_, zu) = project(c0, _IN_CONV)
    b_ref[...] = zb
    u_ref[...] = zc * zu


def _inproj(x, gain, w, batch, seq):
    n = x.shape[0]
    tm = TOKEN_TILE
    tiles_per_seq = seq // tm
    row = lambda i: (i, 0)
    where = lambda i: (i // tiles_per_seq, i % tiles_per_seq)

    def slab_out(width, dtype, d=1):
        n_slabs = width // LANES
        if d == 1:
            return (jax.ShapeDtypeStruct((batch, n_slabs, seq, LANES), dtype),
                    pl.BlockSpec((1, n_slabs, tm, LANES), lambda i: (where(i)[0], 0, where(i)[1], 0)))
        return (jax.ShapeDtypeStruct((batch, n_slabs, d, seq // d, LANES), dtype),
                pl.BlockSpec((1, n_slabs, d, tm // d, LANES), lambda i: (where(i)[0], 0, 0, where(i)[1], 0)))

    flat = [(jax.ShapeDtypeStruct((n, CONV_W), _F32), pl.BlockSpec((tm, CONV_W), row))] * 2
    outs = [slab_out(_IN_ATT, _BF, d) for d in DILATIONS] + [slab_out(_IN_RET, _BF), slab_out(_IN_GATE, _F32)] + flat
    att1, att4, att16, ret, gate, b, u = pl.pallas_call(
        _inproj_kernel,
        out_shape=[o[0] for o in outs],
        grid=(n // tm,),
        in_specs=[pl.BlockSpec((tm, D_MODEL), row), _resident((1, D_MODEL)),
                  _resident((D_MODEL, _IN_TOTAL))],
        out_specs=[o[1] for o in outs],
        scratch_shapes=[pltpu.VMEM((_IN_ATT // LANES, tm, LANES), _F32)] * 2,
        compiler_params=_params(1),
        name="inproj",
    )(x, gain, w)
    att = [a.reshape(batch, _IN_ATT // LANES, seq, LANES) for a in (att1, att4, att16)]
    return att, ret, gate, b, u


ATT_GROUPS = 4
MERGE_UNROLL = 4
ATT_SLOTS = 2
ATT_VARIANTS = 6


def _att_kernel(*refs, seq):
    n_win = len(DILATIONS)
    group = seq // ATT_TQ // ATT_GROUPS
    qkv_refs = [refs[3 * w:3 * w + 3] for w in range(n_win)]
    o_ref, od, ld, bias_ref, s_buf, p_buf, m_buf = refs[3 * n_win:]
    pair = pl.program_id(0)
    lane = lax.broadcasted_iota(jnp.int32, (1, LANES), 1)
    head_a = lane < HEAD_DIM
    zero_bf = jnp.zeros((), _BF)

    mask_a = jnp.broadcast_to(jnp.where(head_a, 1.0, 0.0), (ATT_TK, LANES)).astype(_BF)
    mask_b = jnp.broadcast_to(jnp.where(head_a, 0.0, 1.0), (ATT_TK, LANES)).astype(_BF)

    def heads(x):
        return jnp.where(head_a, x, zero_bf), jnp.where(head_a, zero_bf, x)

    def slope(j):
        s = jnp.float32(_SLOPES[j])
        for p in range(1, ATT_HEADS // 2):
            s = jnp.where(pair == p, jnp.float32(_SLOPES[2 * p + j]), s)
        return s

    @pl.when(pl.program_id(1) == 0)
    def _():
        qi = lax.broadcasted_iota(jnp.int32, (ATT_TQ, ATT_TK), 0)
        kc = lax.broadcasted_iota(jnp.int32, (ATT_TQ, ATT_TK), 1)
        for w, d in enumerate(DILATIONS):
            single = seq // d == ATT_TQ
            variants = {3: (ATT_HALF, True, True)} if single else {
                0: (ATT_HALF, False, False), 1: (ATT_HALF, True, False), 2: (ATT_HALF, False, True)}
            variants[4] = (0, True, single)
            variants[5] = (2 * ATT_HALF, single, True)
            for variant, (shift, first, last) in variants.items():
                rel = jnp.abs(qi + shift - kc)
                valid = rel <= ATT_HALF
                if first:
                    valid = valid & (kc >= shift)
                if last:
                    valid = valid & (kc < shift + ATT_TQ)
                dist = (d * rel).astype(_F32)
                for j in range(2):
                    bias_ref[w, j, variant] = jnp.where(valid, (-slope(j) * dist) * LOG2E, NEG_INF)

    for w, d in enumerate(DILATIONS):
        q_ref, k_ref, v_ref = qkv_refs[w]
        sub_len = seq // d
        n_blocks = sub_len // ATT_TQ
        n_total = d * n_blocks

        def coords(blk):
            r = blk // n_blocks
            b = blk - r * n_blocks
            if isinstance(blk, int):
                at_start, at_end = int(blk == 0), int(blk == n_total - 1)
                variant = 4 if at_start else 5 if at_end else int(b == 0) + 2 * int(b == n_blocks - 1)
            else:
                at_start, at_end = jnp.where(blk == 0, 1, 0), jnp.where(blk == n_total - 1, 1, 0)
                edge = jnp.where(b == 0, 1, 0) + jnp.where(b == n_blocks - 1, 2, 0)
                variant = jnp.where(blk == 0, 4, jnp.where(blk == n_total - 1, 5, edge))
            q0 = _aligned(blk * ATT_TQ, ATT_TQ)
            t0 = _aligned(blk * ATT_TQ - ATT_HALF + ATT_HALF * (at_start - at_end), ATT_HALF)
            return r, b, q0, t0, variant

        def score(grp, slot):
            for u in range(group):
                _, _, q0, t0, variant = coords(grp * group + u)
                q2 = jnp.concatenate(heads(q_ref[0, 0,pl.ds(q0, ATT_TQ), :]), axis=0)
                s = lax.dot_general(q2, k_ref[0, 0,pl.ds(t0, ATT_TK), :], _NT, preferred_element_type=_F32)
                for j in range(2):
                    s_buf[slot, u, j] = s[j * ATT_TQ:(j + 1) * ATT_TQ] + bias_ref[w, j, variant]

        def softmax(slot):
            for u in range(group):
                tops = [jnp.max(s_buf[slot, u, j], axis=-1, keepdims=True) for j in range(2)]
                m_buf[slot, u] = jnp.where(head_a, tops[0], tops[1])
                for j in range(2):
                    p_buf[slot, u, j] = jnp.exp2(s_buf[slot, u, j] - tops[j]).astype(_BF)

        def output(grp, slot):
            for u in range(group):
                r, b, q0, t0, _ = coords(grp * group + u)
                v_a, v_b = heads(v_ref[0, 0,pl.ds(t0, ATT_TK), :])
                acc = jnp.dot(p_buf[slot, u, 0], jnp.concatenate([v_a, mask_a], axis=1),
                              preferred_element_type=_F32)
                acc = acc + jnp.dot(p_buf[slot, u, 1], jnp.concatenate([v_b, mask_b], axis=1),
                                    preferred_element_type=_F32)
                den = acc[:, LANES:]
                dst = pl.ds(r + d * b * ATT_TQ, ATT_TQ, stride=d) if d > 1 else pl.ds(q0, ATT_TQ)
                od[w, dst, :] = acc[:, :LANES] / den
                ld[w, dst, :] = m_buf[slot, u] + jnp.log2(den)

        n_groups = ATT_GROUPS

        score(0, 0)
        softmax(0)
        score(1, 1)

        def steady(t, carry):
            for slot in range(2):
                g = 2 * t + slot
                output(g, slot)
                softmax(1 - slot)
                score(g + 2, slot)
            return carry

        lax.fori_loop(0, n_groups // 2 - 1, steady, 0)
        output(n_groups - 2, 0)
        softmax(1)
        output(n_groups - 1, 1)

    def merge(it, carry):
        for u in range(MERGE_UNROLL):
            rows = pl.ds(pl.multiple_of((it * MERGE_UNROLL + u) * ATT_TQ, ATT_TQ), ATT_TQ)
            lses = [ld[w, rows, :] for w in range(len(DILATIONS))]
            top = jnp.maximum(jnp.maximum(lses[0], lses[1]), lses[2])
            wts = [jnp.exp2(l - top) for l in lses]
            num = wts[0] * od[0, rows, :] + wts[1] * od[1, rows, :] + wts[2] * od[2, rows, :]
            o_ref[0, 0, rows, :] = (num / (wts[0] + wts[1] + wts[2])).astype(o_ref.dtype)
        return carry

    lax.fori_loop(0, seq // ATT_TQ // MERGE_UNROLL, merge, 0)


def _attention(qkv_by_dilation, batch, seq):
    n_pairs = ATT_W // LANES
    group = seq // ATT_TQ // ATT_GROUPS
    blk = (1, 1, seq, LANES)
    part = lambda j: pl.BlockSpec(blk, lambda p, b: (b, j * n_pairs + p, 0, 0))
    return pl.pallas_call(
        functools.partial(_att_kernel, seq=seq),
        out_shape=jax.ShapeDtypeStruct((batch, n_pairs, seq, LANES), _BF),
        grid=(n_pairs, batch),
        in_specs=[part(j) for _ in DILATIONS for j in range(3)],
        out_specs=part(0),
        scratch_shapes=[pltpu.VMEM((len(DILATIONS), seq, LANES), _F32)] * 2
        + [pltpu.VMEM((len(DILATIONS), 2, ATT_VARIANTS, ATT_TQ, ATT_TK), _F32),
           pltpu.VMEM((ATT_SLOTS, group, 2, ATT_TQ, ATT_TK), _F32),
           pltpu.VMEM((ATT_SLOTS, group, 2, ATT_TQ, ATT_TK), _BF),
           pltpu.VMEM((ATT_SLOTS, group, ATT_TQ, LANES), _F32)],
        compiler_params=_params(2),
        name="attention",
    )(*[a for a in qkv_by_dilation for _ in range(3)])


RET_UNROLL = 16
RET_GROUPS = 2


def _ret_kernel(logit_ref, q_ref, k_ref, v_ref, g_ref, o_ref, sf, sb, lhs_buf, mix_buf, *, seq):
    head = pl.program_id(1)
    c = RET_CHUNK
    n_chunks = seq // c
    group = n_chunks // RET_GROUPS
    scale = RET_HEAD_DIM ** -0.5

    def log_sigmoid(x):
        return jnp.minimum(x, 0.0) - jnp.log(1.0 + jnp.exp(-jnp.abs(x)))

    lg_f = log_sigmoid(jnp.full((c, LANES), logit_ref[0, head], _F32))
    lg_b = log_sigmoid(jnp.full((c, LANES), logit_ref[1, head], _F32))
    pos = lax.broadcasted_iota(jnp.int32, (c, LANES), 0).astype(_F32)
    col = lax.broadcasted_iota(jnp.int32, (c, LANES), 1).astype(_F32)
    kw_f = jnp.exp(lg_f * (c - 1.0 - pos)) * scale
    kw_b = jnp.exp(lg_b * pos) * scale
    qw_f = jnp.exp(lg_f * (pos + 1.0))
    qw_b = jnp.exp(lg_b * (c - pos))
    g_f = jnp.exp(lg_f * c)
    g_b = jnp.exp(lg_b * c)
    rel = pos - col
    decay = jnp.where(rel >= 0, jnp.exp(lg_f * jnp.maximum(rel, 0.0)),
                      jnp.exp(lg_b * jnp.maximum(-rel, 0.0))) * scale
    lane = lax.broadcasted_iota(jnp.int32, (1, LANES), 1)
    real = lane < RET_HEAD_DIM

    def increments(it, carry):
        for u in range(RET_UNROLL):
            i = it * RET_UNROLL + u
            t0 = pl.multiple_of(i * c, c)
            k = k_ref[0, 0,pl.ds(t0, c), :].astype(_F32)
            v = v_ref[0, 0,pl.ds(t0, c), :]
            sf[i] = lax.dot_general((k * kw_f).astype(_BF), v, _TN, preferred_element_type=_F32)
            sb[i] = lax.dot_general((k * kw_b).astype(_BF), v, _TN, preferred_element_type=_F32)
        return carry

    lax.fori_loop(0, n_chunks // RET_UNROLL, increments, 0)

    def scan(j, states):
        fwd, bwd = states
        i = n_chunks - 1 - j
        inc_f, inc_b = sf[j], sb[i]
        sf[j] = fwd
        sb[i] = bwd
        return fwd * g_f + inc_f, bwd * g_b + inc_b

    zero_state = jnp.zeros((LANES, LANES), _F32)
    lax.fori_loop(0, n_chunks, scan, (zero_state, zero_state))

    def chunk_rows(grp, u):
        i = grp * group + u
        return i, pl.ds(_aligned(i * c, c), c)

    def weigh(grp, slot):
        for u in range(group):
            _, rows = chunk_rows(grp, u)
            q = q_ref[0, 0,rows, :]
            qf = q.astype(_F32)
            scores = lax.dot_general(q, k_ref[0, 0,rows, :], _NT, preferred_element_type=_F32) * decay
            lhs_buf[slot, u, 0] = scores.astype(_BF)
            lhs_buf[slot, u, 1] = (qf * qw_f).astype(_BF)
            lhs_buf[slot, u, 2] = (qf * qw_b).astype(_BF)

    def mix(grp, slot):
        for u in range(group):
            i, rows = chunk_rows(grp, u)
            o = jnp.dot(lhs_buf[slot, u, 0], v_ref[0, 0,rows, :], preferred_element_type=_F32)
            o = o + jnp.dot(lhs_buf[slot, u, 1], sf[i].astype(_BF), preferred_element_type=_F32)
            o = o + jnp.dot(lhs_buf[slot, u, 2], sb[i].astype(_BF), preferred_element_type=_F32)
            mix_buf[slot, u] = o

    def finish(grp, slot):
        for u in range(group):
            _, rows = chunk_rows(grp, u)
            o = mix_buf[slot, u]
            mu = jnp.sum(o, axis=-1, keepdims=True) * (1.0 / RET_HEAD_DIM)
            dev = jnp.where(real, o - mu, 0.0)
            var = jnp.sum(dev * dev, axis=-1, keepdims=True) * (1.0 / RET_HEAD_DIM)
            g = g_ref[0, 0, rows, :]
            o_ref[0, 0, rows, :] = (dev * lax.rsqrt(var + NORM_EPS) * (g / (1.0 + jnp.exp(-g)))).astype(o_ref.dtype)

    n_groups = RET_GROUPS
    weigh(0, 0)
    mix(0, 0)
    weigh(1, 1)

    def steady(t, carry):
        for slot in range(2):
            grp = 2 * t + slot
            finish(grp, slot)
            mix(grp + 1, 1 - slot)
            weigh(grp + 2, slot)
        return carry

    lax.fori_loop(0, n_groups // 2 - 1, steady, 0)
    finish(n_groups - 2, 0)
    mix(n_groups - 1, 1)
    finish(n_groups - 1, 1)


def _retention(qkv, gate, logits, batch, seq):
    blk = (1, 1, seq, LANES)
    n_chunks = seq // RET_CHUNK
    group = n_chunks // RET_GROUPS
    return pl.pallas_call(
        functools.partial(_ret_kernel, seq=seq),
        out_shape=jax.ShapeDtypeStruct((batch, RET_HEADS, seq, LANES), _BF),
        grid=(batch, RET_HEADS),
        in_specs=[pl.BlockSpec(memory_space=pltpu.SMEM),
                  pl.BlockSpec(blk, lambda b, h: (b, 3 * h, 0, 0)),
                  pl.BlockSpec(blk, lambda b, h: (b, 3 * h + 1, 0, 0)),
                  pl.BlockSpec(blk, lambda b, h: (b, 3 * h + 2, 0, 0)),
                  pl.BlockSpec(blk, lambda b, h: (b, h, 0, 0))],
        out_specs=pl.BlockSpec(blk, lambda b, h: (b, h, 0, 0)),
        scratch_shapes=[pltpu.VMEM((n_chunks, LANES, LANES), _F32)] * 2
        + [pltpu.VMEM((2, group, 3, RET_CHUNK, LANES), _BF),
           pltpu.VMEM((2, group, RET_CHUNK, LANES), _F32)],
        compiler_params=_params(2),
        name="retention",
    )(logits, qkv, qkv, qkv, gate)


def _mix_out_ffn_kernel(att_ref, ret_ref, b_ref, u_ref, up_ref, un_ref, cw_ref, wa_ref, wr_ref, wc_ref,
                        mix_gain_ref, x_ref, ffn_gain_ref, wg_ref, wu_ref, wd_ref, o_ref, *, tiles_per_seq):
    tm = u_ref.shape[0]
    t = pl.program_id(0) % tiles_per_seq
    u = u_ref[...]
    prev_row = jnp.where(t == 0, 0.0, up_ref[7:8, :])
    next_row = jnp.where(t == tiles_per_seq - 1, 0.0, un_ref[0:1, :])
    row_id = lax.broadcasted_iota(jnp.int32, u.shape, 0)
    u_prev = jnp.where(row_id == 0, prev_row, pltpu.roll(u, 1, 0))
    u_next = jnp.where(row_id == tm - 1, next_row, pltpu.roll(u, tm - 1, 0))
    conv = cw_ref[0:1, :] * u_prev + cw_ref[1:2, :] * u + cw_ref[2:3, :] * u_next
    cv = (b_ref[...] * conv).astype(_BF)
    def wide(ref, rows):
        return jnp.concatenate([ref[0, s, rows, :] for s in range(ref.shape[1])], axis=1)

    for rows in _subtiles(tm):
        y = jnp.dot(wide(att_ref, rows), wa_ref[...], preferred_element_type=_F32)
        y = y + jnp.dot(wide(ret_ref, rows), wr_ref[...], preferred_element_type=_F32)
        y = y + jnp.dot(cv[rows], wc_ref[...], preferred_element_type=_F32)
        x_mid = x_ref[rows, :] + _rms(y, mix_gain_ref[...])
        o_ref[rows, :] = _swiglu_residual(x_mid, ffn_gain_ref, wg_ref, wu_ref, wd_ref)


def _mix_out_ffn(att, ret, b, u, conv_w, wa, wr, wc, mix_gain, x, ffn_gains, wg, wu, wd, seq):
    n = x.shape[0]
    tm = FFN_SUBTILES * FFN_SUBTILE
    sub = 8
    row = lambda i: (i, 0)
    prev_blk = lambda i: (jnp.maximum(i * (tm // sub) - 1, 0), 0)
    next_blk = lambda i: (jnp.minimum((i + 1) * (tm // sub), n // sub - 1), 0)
    tiles_per_seq = seq // tm
    slab_rows = lambda i: (i // tiles_per_seq, 0, i % tiles_per_seq, 0)
    return pl.pallas_call(
        functools.partial(_mix_out_ffn_kernel, tiles_per_seq=tiles_per_seq),
        out_shape=jax.ShapeDtypeStruct(x.shape, x.dtype),
        grid=(n // tm,),
        in_specs=[pl.BlockSpec((1, ATT_W // LANES, tm, LANES), slab_rows),
                  pl.BlockSpec((1, RET_HEADS, tm, LANES), slab_rows),
                  pl.BlockSpec((tm, CONV_W), row), pl.BlockSpec((tm, CONV_W), row),
                  pl.BlockSpec((sub, CONV_W), prev_blk), pl.BlockSpec((sub, CONV_W), next_blk),
                  _resident((3, CONV_W)), _resident((ATT_W, D_MODEL)), _resident((RET_PAD_W, D_MODEL)),
                  _resident((CONV_W, D_MODEL)), _resident((1, D_MODEL)),
                  pl.BlockSpec((tm, D_MODEL), row), _resident((2, D_MODEL)),
                  _resident((D_MODEL, D_FF)), _resident((D_MODEL, D_FF)), _resident((D_FF, D_MODEL))],
        out_specs=pl.BlockSpec((tm, D_MODEL), row),
        compiler_params=_params(1),
        name="mix_out_ffn",
    )(att, ret, b, u, u, u, conv_w, wa, wr, wc, mix_gain, x, ffn_gains, wg, wu, wd)


def _pad_heads(w, axis):
    shape = w.shape[:axis] + (RET_HEADS, RET_HEAD_DIM) + w.shape[axis + 1:]
    pad = [(0, 0)] * (len(shape))
    pad[axis + 1] = (0, LANES - RET_HEAD_DIM)
    w = jnp.pad(w.reshape(shape), pad)
    return w.reshape(w.shape[:axis] + (RET_PAD_W,) + w.shape[axis + 2:])


def _mix_in_weights(w):
    att = w[:, :_IN_ATT]
    c0 = _IN_ATT
    parts = [_pad_heads(w[:, c0 + j * RET_W:c0 + (j + 1) * RET_W], 1).reshape(D_MODEL, RET_HEADS, LANES)
             for j in range(4)]
    qkv = jnp.stack(parts[:3], axis=2).reshape(D_MODEL, _IN_RET)
    gate = parts[3].reshape(D_MODEL, _IN_GATE)
    conv = w[:, c0 + 4 * RET_W:]
    return jnp.concatenate([att, qkv, gate, conv], axis=1).astype(_BF)


def _trunk(x, batch, seq, weights):
    for layer in weights:
        x = _ffn(x, layer["gain"][0:2], *layer["ffn1"])
        att_qkv, ret_qkv, gate, b, u = _inproj(x, layer["gain"][2:3], layer["w_in"], batch, seq)
        att = _attention(att_qkv, batch, seq)
        ret = _retention(ret_qkv, gate, layer["logit"], batch, seq)
        x = _mix_out_ffn(att, ret, b, u, layer["conv_w"], *layer["w_out"], layer["gain"][3:4], x,
                         layer["gain"][4:6], *layer["ffn2"], seq)
    return x


def _cast_kernel(w_ref, o_ref):
    o_ref[...] = w_ref[0].astype(o_ref.dtype)


def _layer_bf16(w, layer):
    _, rows, cols = w.shape
    tr = CAST_ROWS
    return pl.pallas_call(
        _cast_kernel,
        out_shape=jax.ShapeDtypeStruct((rows, cols), _BF),
        grid=(rows // tr,),
        in_specs=[pl.BlockSpec((1, tr, cols), lambda i: (layer, i, 0))],
        out_specs=pl.BlockSpec((tr, cols), lambda i: (i, 0)),
        compiler_params=_params(1),
        name="cast_bf16",
    )(w)


def kernel(x_prompt, x_sample, norm_gain, ffn1_w_gate, ffn1_w_up, ffn1_w_down, w_mix_in, conv_w,
           ret_decay_logit, w_mix_out, ffn2_w_gate, ffn2_w_up, ffn2_w_down):
    weights = []
    for l in range(DEPTH):
        wo = w_mix_out[l]
        weights.append(dict(
            gain=norm_gain[l],
            ffn1=tuple(_layer_bf16(w, l) for w in (ffn1_w_gate, ffn1_w_up, ffn1_w_down)),
            ffn2=tuple(_layer_bf16(w, l) for w in (ffn2_w_gate, ffn2_w_up, ffn2_w_down)),
            w_in=_mix_in_weights(w_mix_in[l]),
            conv_w=conv_w[l],
            logit=ret_decay_logit[l],
            w_out=(wo[:ATT_W].astype(_BF), _pad_heads(wo[ATT_W:ATT_W + RET_W], 0).astype(_BF),
                   wo[ATT_W + RET_W:].astype(_BF)),
        ))
    outs = []
    for x in (x_prompt, x_sample):
        batch, seq, _ = x.shape
        outs.append(_trunk(x.reshape(batch * seq, D_MODEL), batch, seq, weights).reshape(x.shape))
    return tuple(outs)
```

```python
import functools
import math

import jax
import jax.numpy as jnp
import numpy as np
from jax import lax
from jax.experimental import pallas as pl
from jax.experimental.pallas import tpu as pltpu

D_MODEL = 1024
D_FF = 2816
DEPTH = 2
HEAD_DIM = 64
ATT_HEADS = 6
ATT_W = ATT_HEADS * HEAD_DIM
DILATIONS = (1, 4, 16)
ATT_HALF = 64
RET_HEADS = 4
RET_HEAD_DIM = 96
RET_W = RET_HEADS * RET_HEAD_DIM
RET_CHUNK = 128
CONV_W = 256
NORM_EPS = 1e-6
NEG_INF = -1e30
LOG2E = math.log2(math.e)
ATT_Q_SCALE = HEAD_DIM ** -0.5 * LOG2E

LANES = 128
RET_PAD_W = RET_HEADS * LANES
ATT_TQ = 128
ATT_TK = ATT_TQ + 2 * ATT_HALF
FF_CHUNK = 256
CAST_ROWS = 256
FFN_SUBTILE = 512
FFN_SUBTILES = 2
VMEM_LIMIT = 56 * 1024 * 1024

_BF = jnp.bfloat16
_F32 = jnp.float32
_NT = (((1,), (1,)), ((), ()))
_TN = (((0,), (0,)), ((), ()))


def _alibi_slope_list(n):
    def pow2(m):
        start = 2.0 ** (-8.0 / m)
        return [start ** (i + 1) for i in range(m)]
    if math.log2(n).is_integer():
        return pow2(n)
    c = 2 ** math.floor(math.log2(n))
    return pow2(c) + _alibi_slope_list(2 * c)[0::2][: n - c]


_SLOPES = [float(np.float32(s)) for s in _alibi_slope_list(ATT_HEADS)]


def _aligned(x, m):
    return x if isinstance(x, int) else pl.multiple_of(x, m)


def _rms(x, g):
    return x * lax.rsqrt(jnp.mean(x * x, axis=-1, keepdims=True) + NORM_EPS) * g


def _params(n_axes):
    return pltpu.CompilerParams(dimension_semantics=("arbitrary",) * n_axes,
                                vmem_limit_bytes=VMEM_LIMIT)


def _resident(shape):
    return pl.BlockSpec(shape, lambda *_: (0,) * len(shape), pipeline_mode=pl.Buffered(1))


def _swiglu_residual(x, gain_ref, wg_ref, wu_ref, wd_ref):
    h = _rms(x, gain_ref[0:1, :]).astype(_BF)
    acc = jnp.zeros(x.shape, _F32)
    for c in range(D_FF // FF_CHUNK):
        sl = slice(c * FF_CHUNK, (c + 1) * FF_CHUNK)
        g = jnp.dot(h, wg_ref[:, sl], preferred_element_type=_F32)
        u = jnp.dot(h, wu_ref[:, sl], preferred_element_type=_F32)
        a = (g / (1.0 + jnp.exp(-g)) * u).astype(_BF)
        acc = acc + jnp.dot(a, wd_ref[sl, :], preferred_element_type=_F32)
    return x + 0.5 * _rms(acc, gain_ref[1:2, :])


def _subtiles(n_rows):
    return [slice(r, r + FFN_SUBTILE) for r in range(0, n_rows, FFN_SUBTILE)]


def _ffn_kernel(x_ref, gain_ref, wg_ref, wu_ref, wd_ref, o_ref):
    for rows in _subtiles(x_ref.shape[0]):
        o_ref[rows, :] = _swiglu_residual(x_ref[rows, :], gain_ref, wg_ref, wu_ref, wd_ref)


def _ffn(x, gains, wg, wu, wd):
    n = x.shape[0]
    tm = FFN_SUBTILES * FFN_SUBTILE
    row = lambda i: (i, 0)
    return pl.pallas_call(
        _ffn_kernel,
        out_shape=jax.ShapeDtypeStruct(x.shape, x.dtype),
        grid=(n // tm,),
        in_specs=[pl.BlockSpec((tm, D_MODEL), row), _resident((2, D_MODEL)),
                  _resident((D_MODEL, D_FF)), _resident((D_MODEL, D_FF)), _resident((D_FF, D_MODEL))],
        out_specs=pl.BlockSpec((tm, D_MODEL), row),
        compiler_params=_params(1),
        name="ffn",
    )(x, gains, wg, wu, wd)


_IN_ATT = 3 * ATT_W
_IN_RET = RET_HEADS * 3 * LANES
_IN_GATE = RET_PAD_W
_IN_CONV = 3 * CONV_W
_IN_TOTAL = _IN_ATT + _IN_RET + _IN_GATE + _IN_CONV
IN_CHUNK = 256


def _inproj_kernel(x_ref, gain_ref, w_ref, att1_ref, att4_ref, att16_ref, ret_ref, gate_ref, b_ref, u_ref,
                   stage, stage4):
    for t, rows in enumerate(_subtiles(x_ref.shape[0])):
        _inproj_subtile(t, rows, x_ref, gain_ref, w_ref, att1_ref, att4_ref, att16_ref, ret_ref, gate_ref, b_ref,
                        u_ref, stage, stage4)


def _inproj_subtile(t, rows, x_ref, gain_ref, w_ref, att1_ref, att4_ref, att16_ref, ret_ref, gate_ref, b_ref,
                    u_ref, stage, stage4):
    tm = FFN_SUBTILE
    h = _rms(x_ref[rows, :], gain_ref[...]).astype(_BF)
    c0 = 0
    def project(start, width):
        for c in range(0, width, IN_CHUNK):
            cw = min(IN_CHUNK, width - c)
            yield c, jnp.dot(h, w_ref[:, start + c:start + c + cw], preferred_element_type=_F32)

    n_slabs = _IN_ATT // LANES
    for c, z in project(c0, _IN_ATT):
        for s in range(c // LANES, (c + z.shape[1]) // LANES):
            zs = z[:, s * LANES - c:(s + 1) * LANES - c]
            if (s + 1) * LANES <= ATT_W:
                zs = zs * ATT_Q_SCALE
            stage[s] = zs
            att1_ref[0, s, rows, :] = zs.astype(_BF)
    d4, d16 = DILATIONS[1], DILATIONS[2]
    rows4, rows16 = tm // d4, tm // d16
    for s in range(n_slabs):
        for r4 in range(d4):
            x = stage[s, pl.ds(r4, rows4, stride=d4), :]
            att4_ref[0, s, r4, t * rows4:(t + 1) * rows4, :] = x.astype(_BF)
            stage4[s, r4 * rows4:(r4 + 1) * rows4, :] = x
        for r4 in range(d4):
            for j in range(d16 // d4):
                x = stage4[s, pl.ds(r4 * rows4 + j, rows16, stride=d16 // d4), :]
                att16_ref[0, s, r4 + d4 * j, t * rows16:(t + 1) * rows16, :] = x.astype(_BF)
    c0 += _IN_ATT

    def slabs(ref, c, z, dtype):
        for s in range(z.shape[1] // LANES):
            ref[0, c // LANES + s, rows, :] = z[:, s * LANES:(s + 1) * LANES].astype(dtype)

    for c, z in project(c0, _IN_RET):
        slabs(ret_ref, c, z, _BF)
    c0 += _IN_RET
    for c, z in project(c0, _IN_GATE):
        slabs(gate_ref, c, z, _F32)
    c0 += _IN_GATE
    (_, zb), (_, zc), (_, zu) = project(c0, _IN_CONV)
    b_ref[rows, :] = zb
    u_ref[rows, :] = zc * zu


def _inproj(x, gain, w, batch, seq):
    n = x.shape[0]
    tm = FFN_SUBTILES * FFN_SUBTILE
    tiles_per_seq = seq // tm
    row = lambda i: (i, 0)
    where = lambda i: (i // tiles_per_seq, i % tiles_per_seq)

    def slab_out(width, dtype, d=1):
        n_slabs = width // LANES
        if d == 1:
            return (jax.ShapeDtypeStruct((batch, n_slabs, seq, LANES), dtype),
                    pl.BlockSpec((1, n_slabs, tm, LANES), lambda i: (where(i)[0], 0, where(i)[1], 0)))
        return (jax.ShapeDtypeStruct((batch, n_slabs, d, seq // d, LANES), dtype),
                pl.BlockSpec((1, n_slabs, d, tm // d, LANES), lambda i: (where(i)[0], 0, 0, where(i)[1], 0)))

    flat = [(jax.ShapeDtypeStruct((n, CONV_W), _F32), pl.BlockSpec((tm, CONV_W), row))] * 2
    outs = [slab_out(_IN_ATT, _BF, d) for d in DILATIONS] + [slab_out(_IN_RET, _BF), slab_out(_IN_GATE, _F32)] + flat
    att1, att4, att16, ret, gate, b, u = pl.pallas_call(
        _inproj_kernel,
        out_shape=[o[0] for o in outs],
        grid=(n // tm,),
        in_specs=[pl.BlockSpec((tm, D_MODEL), row), _resident((1, D_MODEL)),
                  _resident((D_MODEL, _IN_TOTAL))],
        out_specs=[o[1] for o in outs],
        scratch_shapes=[pltpu.VMEM((_IN_ATT // LANES, FFN_SUBTILE, LANES), _F32)] * 2,
        compiler_params=_params(1),
        name="inproj",
    )(x, gain, w)
    att = [a.reshape(batch, _IN_ATT // LANES, seq, LANES) for a in (att1, att4, att16)]
    return att, ret, gate, b, u


ATT_GROUPS = 4
MERGE_UNROLL = 8
ATT_SLOTS = 2
ATT_VARIANTS = 6


def _att_kernel(*refs, seq):
    n_win = len(DILATIONS)
    group = seq // ATT_TQ // ATT_GROUPS
    qkv_refs = [refs[3 * w:3 * w + 3] for w in range(n_win)]
    o_ref, od, ld, bias_ref, s_buf, p_buf, m_buf = refs[3 * n_win:]
    pair = pl.program_id(0)
    lane = lax.broadcasted_iota(jnp.int32, (1, LANES), 1)
    head_a = lane < HEAD_DIM
    zero_bf = jnp.zeros((), _BF)

    mask_a = jnp.broadcast_to(jnp.where(head_a, 1.0, 0.0), (ATT_TK, LANES)).astype(_BF)
    mask_b = jnp.broadcast_to(jnp.where(head_a, 0.0, 1.0), (ATT_TK, LANES)).astype(_BF)

    def heads(x):
        return jnp.where(head_a, x, zero_bf), jnp.where(head_a, zero_bf, x)

    def slope(j):
        s = jnp.float32(_SLOPES[j])
        for p in range(1, ATT_HEADS // 2):
            s = jnp.where(pair == p, jnp.float32(_SLOPES[2 * p + j]), s)
        return s

    @pl.when(pl.program_id(1) == 0)
    def _():
        qi = lax.broadcasted_iota(jnp.int32, (ATT_TQ, ATT_TK), 0)
        kc = lax.broadcasted_iota(jnp.int32, (ATT_TQ, ATT_TK), 1)
        for w, d in enumerate(DILATIONS):
            single = seq // d == ATT_TQ
            variants = {3: (ATT_HALF, True, True)} if single else {
                0: (ATT_HALF, False, False), 1: (ATT_HALF, True, False), 2: (ATT_HALF, False, True)}
            variants[4] = (0, True, single)
            variants[5] = (2 * ATT_HALF, single, True)
            for variant, (shift, first, last) in variants.items():
                rel = jnp.abs(qi + shift - kc)
                valid = rel <= ATT_HALF
                if first:
                    valid = valid & (kc >= shift)
                if last:
                    valid = valid & (kc < shift + ATT_TQ)
                dist = (d * rel).astype(_F32)
                for j in range(2):
                    bias_ref[w, j, variant] = jnp.where(valid, (-slope(j) * dist) * LOG2E, NEG_INF)

    for w, d in enumerate(DILATIONS):
        q_ref, k_ref, v_ref = qkv_refs[w]
        sub_len = seq // d
        n_blocks = sub_len // ATT_TQ
        n_total = d * n_blocks

        def coords(blk):
            r = blk // n_blocks
            b = blk - r * n_blocks
            if isinstance(blk, int):
                at_start, at_end = int(blk == 0), int(blk == n_total - 1)
                variant = 4 if at_start else 5 if at_end else int(b == 0) + 2 * int(b == n_blocks - 1)
            else:
                at_start, at_end = jnp.where(blk == 0, 1, 0), jnp.where(blk == n_total - 1, 1, 0)
                edge = jnp.where(b == 0, 1, 0) + jnp.where(b == n_blocks - 1, 2, 0)
                variant = jnp.where(blk == 0, 4, jnp.where(blk == n_total - 1, 5, edge))
            q0 = _aligned(blk * ATT_TQ, ATT_TQ)
            t0 = _aligned(blk * ATT_TQ - ATT_HALF + ATT_HALF * (at_start - at_end), ATT_HALF)
            return r, b, q0, t0, variant

        def score(grp, slot):
            for u in range(group):
                _, _, q0, t0, variant = coords(grp * group + u)
                q2 = jnp.concatenate(heads(q_ref[0, 0,pl.ds(q0, ATT_TQ), :]), axis=0)
                s = lax.dot_general(q2, k_ref[0, 0,pl.ds(t0, ATT_TK), :], _NT, preferred_element_type=_F32)
                for j in range(2):
                    s_buf[slot, u, j] = s[j * ATT_TQ:(j + 1) * ATT_TQ] + bias_ref[w, j, variant]

        def softmax(slot):
            for u in range(group):
                tops = [jnp.max(s_buf[slot, u, j], axis=-1, keepdims=True) for j in range(2)]
                m_buf[slot, u] = jnp.where(head_a, tops[0], tops[1])
                for j in range(2):
                    p_buf[slot, u, j] = jnp.exp2(s_buf[slot, u, j] - tops[j]).astype(_BF)

        def output(grp, slot):
            for u in range(group):
                r, b, q0, t0, _ = coords(grp * group + u)
                v_a, v_b = heads(v_ref[0, 0,pl.ds(t0, ATT_TK), :])
                acc = jnp.dot(p_buf[slot, u, 0], jnp.concatenate([v_a, mask_a], axis=1),
                              preferred_element_type=_F32)
                acc = acc + jnp.dot(p_buf[slot, u, 1], jnp.concatenate([v_b, mask_b], axis=1),
                                    preferred_element_type=_F32)
                den = acc[:, LANES:]
                dst = pl.ds(r + d * b * ATT_TQ, ATT_TQ, stride=d) if d > 1 else pl.ds(q0, ATT_TQ)
                od[w, dst, :] = acc[:, :LANES] / den
                ld[w, dst, :] = m_buf[slot, u] + jnp.log2(den)

        n_groups = ATT_GROUPS

        score(0, 0)
        softmax(0)
        score(1, 1)

        def steady(t, carry):
            for slot in range(2):
                g = 2 * t + slot
                output(g, slot)
                softmax(1 - slot)
                score(g + 2, slot)
            return carry

        lax.fori_loop(0, n_groups // 2 - 1, steady, 0)
        output(n_groups - 2, 0)
        softmax(1)
        output(n_groups - 1, 1)

    def merge(it, carry):
        for u in range(MERGE_UNROLL):
            rows = pl.ds(pl.multiple_of((it * MERGE_UNROLL + u) * ATT_TQ, ATT_TQ), ATT_TQ)
            lses = [ld[w, rows, :] for w in range(len(DILATIONS))]
            top = jnp.maximum(jnp.maximum(lses[0], lses[1]), lses[2])
            wts = [jnp.exp2(l - top) for l in lses]
            num = wts[0] * od[0, rows, :] + wts[1] * od[1, rows, :] + wts[2] * od[2, rows, :]
            o_ref[0, 0, rows, :] = (num / (wts[0] + wts[1] + wts[2])).astype(o_ref.dtype)
        return carry

    lax.fori_loop(0, seq // ATT_TQ // MERGE_UNROLL, merge, 0)


def _attention(qkv_by_dilation, batch, seq):
    n_pairs = ATT_W // LANES
    group = seq // ATT_TQ // ATT_GROUPS
    blk = (1, 1, seq, LANES)
    part = lambda j: pl.BlockSpec(blk, lambda p, b: (b, j * n_pairs + p, 0, 0))
    return pl.pallas_call(
        functools.partial(_att_kernel, seq=seq),
        out_shape=jax.ShapeDtypeStruct((batch, n_pairs, seq, LANES), _BF),
        grid=(n_pairs, batch),
        in_specs=[part(j) for _ in DILATIONS for j in range(3)],
        out_specs=part(0),
        scratch_shapes=[pltpu.VMEM((len(DILATIONS), seq, LANES), _F32)] * 2
        + [pltpu.VMEM((len(DILATIONS), 2, ATT_VARIANTS, ATT_TQ, ATT_TK), _F32),
           pltpu.VMEM((ATT_SLOTS, group, 2, ATT_TQ, ATT_TK), _F32),
           pltpu.VMEM((ATT_SLOTS, group, 2, ATT_TQ, ATT_TK), _BF),
           pltpu.VMEM((ATT_SLOTS, group, ATT_TQ, LANES), _F32)],
        compiler_params=_params(2),
        name="attention",
    )(*[a for a in qkv_by_dilation for _ in range(3)])


RET_GROUPS = 2


def _ret_kernel(logit_ref, q_ref, k_ref, v_ref, g_ref, o_ref, sf, sb, lhs_buf, mix_buf, *, seq):
    head = pl.program_id(1)
    c = RET_CHUNK
    n_chunks = seq // c
    group = n_chunks // RET_GROUPS
    scale = RET_HEAD_DIM ** -0.5

    def log_sigmoid(x):
        return jnp.minimum(x, 0.0) - jnp.log(1.0 + jnp.exp(-jnp.abs(x)))

    lg_f = log_sigmoid(jnp.full((c, LANES), logit_ref[0, head], _F32))
    lg_b = log_sigmoid(jnp.full((c, LANES), logit_ref[1, head], _F32))
    pos = lax.broadcasted_iota(jnp.int32, (c, LANES), 0).astype(_F32)
    col = lax.broadcasted_iota(jnp.int32, (c, LANES), 1).astype(_F32)
    kw_f = jnp.exp(lg_f * (c - 1.0 - pos)) * scale
    kw_b = jnp.exp(lg_b * pos) * scale
    qw_f = jnp.exp(lg_f * (pos + 1.0))
    qw_b = jnp.exp(lg_b * (c - pos))
    g_f = jnp.exp(lg_f * c)
    g_b = jnp.exp(lg_b * c)
    rel = pos - col
    decay = jnp.where(rel >= 0, jnp.exp(lg_f * jnp.maximum(rel, 0.0)),
                      jnp.exp(lg_b * jnp.maximum(-rel, 0.0))) * scale
    lane = lax.broadcasted_iota(jnp.int32, (1, LANES), 1)
    real = lane < RET_HEAD_DIM

    for i in range(n_chunks):
        k = k_ref[0, 0, i * c:(i + 1) * c, :].astype(_F32)
        v = v_ref[0, 0, i * c:(i + 1) * c, :]
        sf[i] = lax.dot_general((k * kw_f).astype(_BF), v, _TN, preferred_element_type=_F32)
        sb[i] = lax.dot_general((k * kw_b).astype(_BF), v, _TN, preferred_element_type=_F32)

    def scan(j, states):
        fwd, bwd = states
        i = n_chunks - 1 - j
        inc_f, inc_b = sf[j], sb[i]
        sf[j] = fwd
        sb[i] = bwd
        return fwd * g_f + inc_f, bwd * g_b + inc_b

    zero_state = jnp.zeros((LANES, LANES), _F32)
    lax.fori_loop(0, n_chunks, scan, (zero_state, zero_state))

    def chunk_rows(grp, u):
        i = grp * group + u
        return i, pl.ds(_aligned(i * c, c), c)

    def weigh(grp, slot):
        for u in range(group):
            _, rows = chunk_rows(grp, u)
            q = q_ref[0, 0,rows, :]
            qf = q.astype(_F32)
            scores = lax.dot_general(q, k_ref[0, 0,rows, :], _NT, preferred_element_type=_F32) * decay
            lhs_buf[slot, u, 0] = scores.astype(_BF)
            lhs_buf[slot, u, 1] = (qf * qw_f).astype(_BF)
            lhs_buf[slot, u, 2] = (qf * qw_b).astype(_BF)

    def mix(grp, slot):
        for u in range(group):
            i, rows = chunk_rows(grp, u)
            o = jnp.dot(lhs_buf[slot, u, 0], v_ref[0, 0,rows, :], preferred_element_type=_F32)
            o = o + jnp.dot(lhs_buf[slot, u, 1], sf[i].astype(_BF), preferred_element_type=_F32)
            o = o + jnp.dot(lhs_buf[slot, u, 2], sb[i].astype(_BF), preferred_element_type=_F32)
            mix_buf[slot, u] = o

    def finish(grp, slot):
        for u in range(group):
            _, rows = chunk_rows(grp, u)
            o = mix_buf[slot, u]
            mu = jnp.sum(o, axis=-1, keepdims=True) * (1.0 / RET_HEAD_DIM)
            dev = jnp.where(real, o - mu, 0.0)
            var = jnp.sum(dev * dev, axis=-1, keepdims=True) * (1.0 / RET_HEAD_DIM)
            g = g_ref[0, 0, rows, :]
            o_ref[0, 0, rows, :] = (dev * lax.rsqrt(var + NORM_EPS) * (g / (1.0 + jnp.exp(-g)))).astype(o_ref.dtype)

    n_groups = RET_GROUPS
    weigh(0, 0)
    mix(0, 0)
    weigh(1, 1)

    def steady(t, carry):
        for slot in range(2):
            grp = 2 * t + slot
            finish(grp, slot)
            mix(grp + 1, 1 - slot)
            weigh(grp + 2, slot)
        return carry

    lax.fori_loop(0, n_groups // 2 - 1, steady, 0)
    finish(n_groups - 2, 0)
    mix(n_groups - 1, 1)
    finish(n_groups - 1, 1)


def _retention(qkv, gate, logits, batch, seq):
    blk = (1, 1, seq, LANES)
    n_chunks = seq // RET_CHUNK
    group = n_chunks // RET_GROUPS
    return pl.pallas_call(
        functools.partial(_ret_kernel, seq=seq),
        out_shape=jax.ShapeDtypeStruct((batch, RET_HEADS, seq, LANES), _BF),
        grid=(batch, RET_HEADS),
        in_specs=[pl.BlockSpec(memory_space=pltpu.SMEM),
                  pl.BlockSpec(blk, lambda b, h: (b, 3 * h, 0, 0)),
                  pl.BlockSpec(blk, lambda b, h: (b, 3 * h + 1, 0, 0)),
                  pl.BlockSpec(blk, lambda b, h: (b, 3 * h + 2, 0, 0)),
                  pl.BlockSpec(blk, lambda b, h: (b, h, 0, 0))],
        out_specs=pl.BlockSpec(blk, lambda b, h: (b, h, 0, 0)),
        scratch_shapes=[pltpu.VMEM((n_chunks, LANES, LANES), _F32)] * 2
        + [pltpu.VMEM((2, group, 3, RET_CHUNK, LANES), _BF),
           pltpu.VMEM((2, group, RET_CHUNK, LANES), _F32)],
        compiler_params=_params(2),
        name="retention",
    )(logits, qkv, qkv, qkv, gate)


def _mix_out_ffn_kernel(att_ref, ret_ref, b_ref, u_ref, up_ref, un_ref, cw_ref, wa_ref, wr_ref, wc_ref,
                        mix_gain_ref, x_ref, ffn_gain_ref, wg_ref, wu_ref, wd_ref, o_ref, *, tiles_per_seq):
    tm = u_ref.shape[0]
    t = pl.program_id(0) % tiles_per_seq
    u = u_ref[...]
    prev_row = jnp.where(t == 0, 0.0, up_ref[7:8, :])
    next_row = jnp.where(t == tiles_per_seq - 1, 0.0, un_ref[0:1, :])
    row_id = lax.broadcasted_iota(jnp.int32, u.shape, 0)
    u_prev = jnp.where(row_id == 0, prev_row, pltpu.roll(u, 1, 0))
    u_next = jnp.where(row_id == tm - 1, next_row, pltpu.roll(u, tm - 1, 0))
    conv = cw_ref[0:1, :] * u_prev + cw_ref[1:2, :] * u + cw_ref[2:3, :] * u_next
    cv = (b_ref[...] * conv).astype(_BF)
    def wide(ref, rows):
        return jnp.concatenate([ref[0, s, rows, :] for s in range(ref.shape[1])], axis=1)

    for rows in _subtiles(tm):
        y = jnp.dot(wide(att_ref, rows), wa_ref[...], preferred_element_type=_F32)
        y = y + jnp.dot(wide(ret_ref, rows), wr_ref[...], preferred_element_type=_F32)
        y = y + jnp.dot(cv[rows], wc_ref[...], preferred_element_type=_F32)
        x_mid = x_ref[rows, :] + _rms(y, mix_gain_ref[...])
        o_ref[rows, :] = _swiglu_residual(x_mid, ffn_gain_ref, wg_ref, wu_ref, wd_ref)


def _mix_out_ffn(att, ret, b, u, conv_w, wa, wr, wc, mix_gain, x, ffn_gains, wg, wu, wd, seq):
    n = x.shape[0]
    tm = FFN_SUBTILES * FFN_SUBTILE
    sub = 8
    row = lambda i: (i, 0)
    prev_blk = lambda i: (jnp.maximum(i * (tm // sub) - 1, 0), 0)
    next_blk = lambda i: (jnp.minimum((i + 1) * (tm // sub), n // sub - 1), 0)
    tiles_per_seq = seq // tm
    slab_rows = lambda i: (i // tiles_per_seq, 0, i % tiles_per_seq, 0)
    return pl.pallas_call(
        functools.partial(_mix_out_ffn_kernel, tiles_per_seq=tiles_per_seq),
        out_shape=jax.ShapeDtypeStruct(x.shape, x.dtype),
        grid=(n // tm,),
        in_specs=[pl.BlockSpec((1, ATT_W // LANES, tm, LANES), slab_rows),
                  pl.BlockSpec((1, RET_HEADS, tm, LANES), slab_rows),
                  pl.BlockSpec((tm, CONV_W), row), pl.BlockSpec((tm, CONV_W), row),
                  pl.BlockSpec((sub, CONV_W), prev_blk), pl.BlockSpec((sub, CONV_W), next_blk),
                  _resident((3, CONV_W)), _resident((ATT_W, D_MODEL)), _resident((RET_PAD_W, D_MODEL)),
                  _resident((CONV_W, D_MODEL)), _resident((1, D_MODEL)),
                  pl.BlockSpec((tm, D_MODEL), row), _resident((2, D_MODEL)),
                  _resident((D_MODEL, D_FF)), _resident((D_MODEL, D_FF)), _resident((D_FF, D_MODEL))],
        out_specs=pl.BlockSpec((tm, D_MODEL), row),
        compiler_params=_params(1),
        name="mix_out_ffn",
    )(att, ret, b, u, u, u, conv_w, wa, wr, wc, mix_gain, x, ffn_gains, wg, wu, wd)


def _pad_heads(w, axis):
    shape = w.shape[:axis] + (RET_HEADS, RET_HEAD_DIM) + w.shape[axis + 1:]
    pad = [(0, 0)] * (len(shape))
    pad[axis + 1] = (0, LANES - RET_HEAD_DIM)
    w = jnp.pad(w.reshape(shape), pad)
    return w.reshape(w.shape[:axis] + (RET_PAD_W,) + w.shape[axis + 2:])


def _mix_in_weights(w):
    att = w[:, :_IN_ATT]
    c0 = _IN_ATT
    parts = [_pad_heads(w[:, c0 + j * RET_W:c0 + (j + 1) * RET_W], 1).reshape(D_MODEL, RET_HEADS, LANES)
             for j in range(4)]
    qkv = jnp.stack(parts[:3], axis=2).reshape(D_MODEL, _IN_RET)
    gate = parts[3].reshape(D_MODEL, _IN_GATE)
    conv = w[:, c0 + 4 * RET_W:]
    return jnp.concatenate([att, qkv, gate, conv], axis=1).astype(_BF)


def _trunk(x, batch, seq, weights):
    for layer in weights:
        x = _ffn(x, layer["gain"][0:2], *layer["ffn1"])
        att_qkv, ret_qkv, gate, b, u = _inproj(x, layer["gain"][2:3], layer["w_in"], batch, seq)
        att = _attention(att_qkv, batch, seq)
        ret = _retention(ret_qkv, gate, layer["logit"], batch, seq)
        x = _mix_out_ffn(att, ret, b, u, layer["conv_w"], *layer["w_out"], layer["gain"][3:4], x,
                         layer["gain"][4:6], *layer["ffn2"], seq)
    return x


def _cast_kernel(w_ref, o_ref):
    o_ref[...] = w_ref[0].astype(o_ref.dtype)


def _layer_bf16(w, layer):
    _, rows, cols = w.shape
    tr = CAST_ROWS
    return pl.pallas_call(
        _cast_kernel,
        out_shape=jax.ShapeDtypeStruct((rows, cols), _BF),
        grid=(rows // tr,),
        in_specs=[pl.BlockSpec((1, tr, cols), lambda i: (layer, i, 0))],
        out_specs=pl.BlockSpec((tr, cols), lambda i: (i, 0)),
        compiler_params=_params(1),
        name="cast_bf16",
    )(w)


def kernel(x_prompt, x_sample, norm_gain, ffn1_w_gate, ffn1_w_up, ffn1_w_down, w_mix_in, conv_w,
           ret_decay_logit, w_mix_out, ffn2_w_gate, ffn2_w_up, ffn2_w_down):
    weights = []
    for l in range(DEPTH):
        wo = w_mix_out[l]
        weights.append(dict(
            gain=norm_gain[l],
            ffn1=tuple(_layer_bf16(w, l) for w in (ffn1_w_gate, ffn1_w_up, ffn1_w_down)),
            ffn2=tuple(_layer_bf16(w, l) for w in (ffn2_w_gate, ffn2_w_up, ffn2_w_down)),
            w_in=_mix_in_weights(w_mix_in[l]),
            conv_w=conv_w[l],
            logit=ret_decay_logit[l],
            w_out=(wo[:ATT_W].astype(_BF), _pad_heads(wo[ATT_W:ATT_W + RET_W], 0).astype(_BF),
                   wo[ATT_W + RET_W:].astype(_BF)),
        ))
    outs = []
    for x in (x_prompt, x_sample):
        batch, seq, _ = x.shape
        outs.append(_trunk(x.reshape(batch * seq, D_MODEL), batch, seq, weights).reshape(x.shape))
    return tuple(outs)
```

```python
import functools
import math

import jax
import jax.numpy as jnp
import numpy as np
from jax import lax
from jax.experimental import pallas as pl
from jax.experimental.pallas import tpu as pltpu

D_MODEL = 1024
D_FF = 2816
DEPTH = 2
HEAD_DIM = 64
ATT_HEADS = 6
ATT_W = ATT_HEADS * HEAD_DIM
DILATIONS = (1, 4, 16)
ATT_HALF = 64
RET_HEADS = 4
RET_HEAD_DIM = 96
RET_W = RET_HEADS * RET_HEAD_DIM
RET_CHUNK = 128
CONV_W = 256
NORM_EPS = 1e-6
NEG_INF = -1e30
LOG2E = math.log2(math.e)
ATT_Q_SCALE = HEAD_DIM ** -0.5 * LOG2E

LANES = 128
RET_PAD_W = RET_HEADS * LANES
ATT_TQ = 128
ATT_TK = ATT_TQ + 2 * ATT_HALF
FF_CHUNK = 256
FFN_SUBTILE = 512
FFN_SUBTILES = 2
VMEM_LIMIT = 56 * 1024 * 1024

_BF = jnp.bfloat16
_F32 = jnp.float32
_NT = (((1,), (1,)), ((), ()))
_TN = (((0,), (0,)), ((), ()))


def _alibi_slope_list(n):
    def pow2(m):
        start = 2.0 ** (-8.0 / m)
        return [start ** (i + 1) for i in range(m)]
    if math.log2(n).is_integer():
        return pow2(n)
    c = 2 ** math.floor(math.log2(n))
    return pow2(c) + _alibi_slope_list(2 * c)[0::2][: n - c]


_SLOPES = [float(np.float32(s)) for s in _alibi_slope_list(ATT_HEADS)]


def _aligned(x, m):
    return x if isinstance(x, int) else pl.multiple_of(x, m)


def _rms(x, g):
    return x * lax.rsqrt(jnp.mean(x * x, axis=-1, keepdims=True) + NORM_EPS) * g


def _params(n_axes):
    return pltpu.CompilerParams(dimension_semantics=("arbitrary",) * n_axes,
                                vmem_limit_bytes=VMEM_LIMIT)


def _resident(shape):
    return pl.BlockSpec(shape, lambda *_: (0,) * len(shape), pipeline_mode=pl.Buffered(1))


def _swiglu_residual(x, gain_ref, wg_ref, wu_ref, wd_ref):
    h = _rms(x, gain_ref[0:1, :]).astype(_BF)
    acc = jnp.zeros(x.shape, _F32)
    for c in range(D_FF // FF_CHUNK):
        sl = slice(c * FF_CHUNK, (c + 1) * FF_CHUNK)
        g = jnp.dot(h, wg_ref[:, sl], preferred_element_type=_F32)
        u = jnp.dot(h, wu_ref[:, sl], preferred_element_type=_F32)
        a = (g / (1.0 + jnp.exp(-g)) * u).astype(_BF)
        acc = acc + jnp.dot(a, wd_ref[sl, :], preferred_element_type=_F32)
    return x + 0.5 * _rms(acc, gain_ref[1:2, :])


def _subtiles(n_rows):
    return [slice(r, r + FFN_SUBTILE) for r in range(0, n_rows, FFN_SUBTILE)]


def _ffn_kernel(x_ref, gain_ref, wg_ref, wu_ref, wd_ref, o_ref):
    for rows in _subtiles(x_ref.shape[0]):
        o_ref[rows, :] = _swiglu_residual(x_ref[rows, :], gain_ref, wg_ref, wu_ref, wd_ref)


def _ffn(x, gains, wg, wu, wd):
    n = x.shape[0]
    tm = FFN_SUBTILES * FFN_SUBTILE
    row = lambda i: (i, 0)
    return pl.pallas_call(
        _ffn_kernel,
        out_shape=jax.ShapeDtypeStruct(x.shape, x.dtype),
        grid=(n // tm,),
        in_specs=[pl.BlockSpec((tm, D_MODEL), row), _resident((2, D_MODEL)),
                  _resident((D_MODEL, D_FF)), _resident((D_MODEL, D_FF)), _resident((D_FF, D_MODEL))],
        out_specs=pl.BlockSpec((tm, D_MODEL), row),
        compiler_params=_params(1),
        name="ffn",
    )(x, gains, wg, wu, wd)


_IN_ATT = 3 * ATT_W
_IN_RET = RET_HEADS * 3 * LANES
_IN_GATE = RET_PAD_W
_IN_CONV = 3 * CONV_W
_IN_TOTAL = _IN_ATT + _IN_RET + _IN_GATE + _IN_CONV
IN_CHUNK = 256


def _inproj_kernel(x_ref, gain_ref, w_ref, att1_ref, att4_ref, att16_ref, ret_ref, gate_ref, b_ref, u_ref,
                   stage, stage4):
    for t, rows in enumerate(_subtiles(x_ref.shape[0])):
        _inproj_subtile(t, rows, x_ref, gain_ref, w_ref, att1_ref, att4_ref, att16_ref, ret_ref, gate_ref, b_ref,
                        u_ref, stage, stage4)


def _inproj_subtile(t, rows, x_ref, gain_ref, w_ref, att1_ref, att4_ref, att16_ref, ret_ref, gate_ref, b_ref,
                    u_ref, stage, stage4):
    tm = FFN_SUBTILE
    h = _rms(x_ref[rows, :], gain_ref[...]).astype(_BF)
    c0 = 0
    def project(start, width):
        for c in range(0, width, IN_CHUNK):
            cw = min(IN_CHUNK, width - c)
            yield c, jnp.dot(h, w_ref[:, start + c:start + c + cw], preferred_element_type=_F32)

    n_slabs = _IN_ATT // LANES
    for c, z in project(c0, _IN_ATT):
        for s in range(c // LANES, (c + z.shape[1]) // LANES):
            zs = z[:, s * LANES - c:(s + 1) * LANES - c]
            if (s + 1) * LANES <= ATT_W:
                zs = zs * ATT_Q_SCALE
            stage[s] = zs
            att1_ref[0, s, rows, :] = zs.astype(_BF)
    d4, d16 = DILATIONS[1], DILATIONS[2]
    rows4, rows16 = tm // d4, tm // d16
    for s in range(n_slabs):
        for r4 in range(d4):
            x = stage[s, pl.ds(r4, rows4, stride=d4), :]
            att4_ref[0, s, r4, t * rows4:(t + 1) * rows4, :] = x.astype(_BF)
            stage4[s, r4 * rows4:(r4 + 1) * rows4, :] = x
        for r4 in range(d4):
            for j in range(d16 // d4):
                x = stage4[s, pl.ds(r4 * rows4 + j, rows16, stride=d16 // d4), :]
                att16_ref[0, s, r4 + d4 * j, t * rows16:(t + 1) * rows16, :] = x.astype(_BF)
    c0 += _IN_ATT

    def slabs(ref, c, z, dtype):
        for s in range(z.shape[1] // LANES):
            ref[0, c // LANES + s, rows, :] = z[:, s * LANES:(s + 1) * LANES].astype(dtype)

    for c, z in project(c0, _IN_RET):
        slabs(ret_ref, c, z, _BF)
    c0 += _IN_RET
    for c, z in project(c0, _IN_GATE):
        slabs(gate_ref, c, z, _F32)
    c0 += _IN_GATE
    (_, zb), (_, zc), (_, zu) = project(c0, _IN_CONV)
    b_ref[rows, :] = zb
    u_ref[rows, :] = zc * zu


def _inproj(x, gain, w, batch, seq):
    n = x.shape[0]
    tm = FFN_SUBTILES * FFN_SUBTILE
    tiles_per_seq = seq // tm
    row = lambda i: (i, 0)
    where = lambda i: (i // tiles_per_seq, i % tiles_per_seq)

    def slab_out(width, dtype, d=1):
        n_slabs = width // LANES
        if d == 1:
            return (jax.ShapeDtypeStruct((batch, n_slabs, seq, LANES), dtype),
                    pl.BlockSpec((1, n_slabs, tm, LANES), lambda i: (where(i)[0], 0, where(i)[1], 0)))
        return (jax.ShapeDtypeStruct((batch, n_slabs, d, seq // d, LANES), dtype),
                pl.BlockSpec((1, n_slabs, d, tm // d, LANES), lambda i: (where(i)[0], 0, 0, where(i)[1], 0)))

    flat = [(jax.ShapeDtypeStruct((n, CONV_W), _F32), pl.BlockSpec((tm, CONV_W), row))] * 2
    outs = [slab_out(_IN_ATT, _BF, d) for d in DILATIONS] + [slab_out(_IN_RET, _BF), slab_out(_IN_GATE, _F32)] + flat
    att1, att4, att16, ret, gate, b, u = pl.pallas_call(
        _inproj_kernel,
        out_shape=[o[0] for o in outs],
        grid=(n // tm,),
        in_specs=[pl.BlockSpec((tm, D_MODEL), row), _resident((1, D_MODEL)),
                  _resident((D_MODEL, _IN_TOTAL))],
        out_specs=[o[1] for o in outs],
        scratch_shapes=[pltpu.VMEM((_IN_ATT // LANES, FFN_SUBTILE, LANES), _F32)] * 2,
        compiler_params=_params(1),
        name="inproj",
    )(x, gain, w)
    att = [a.reshape(batch, _IN_ATT // LANES, seq, LANES) for a in (att1, att4, att16)]
    return att, ret, gate, b, u


ATT_GROUPS = 4
MERGE_UNROLL = 8
ATT_SLOTS = 2
ATT_VARIANTS = 6


def _att_kernel(*refs, seq):
    n_win = len(DILATIONS)
    group = seq // ATT_TQ // ATT_GROUPS
    qkv_refs = [refs[3 * w:3 * w + 3] for w in range(n_win)]
    o_ref, od, ld, bias_ref, s_buf, p_buf, m_buf = refs[3 * n_win:]
    pair = pl.program_id(0)
    lane = lax.broadcasted_iota(jnp.int32, (1, LANES), 1)
    head_a = lane < HEAD_DIM
    zero_bf = jnp.zeros((), _BF)

    mask_a = jnp.broadcast_to(jnp.where(head_a, 1.0, 0.0), (ATT_TK, LANES)).astype(_BF)
    mask_b = jnp.broadcast_to(jnp.where(head_a, 0.0, 1.0), (ATT_TK, LANES)).astype(_BF)

    def heads(x):
        return jnp.where(head_a, x, zero_bf), jnp.where(head_a, zero_bf, x)

    def slope(j):
        s = jnp.float32(_SLOPES[j])
        for p in range(1, ATT_HEADS // 2):
            s = jnp.where(pair == p, jnp.float32(_SLOPES[2 * p + j]), s)
        return s

    @pl.when(pl.program_id(1) == 0)
    def _():
        qi = lax.broadcasted_iota(jnp.int32, (ATT_TQ, ATT_TK), 0)
        kc = lax.broadcasted_iota(jnp.int32, (ATT_TQ, ATT_TK), 1)
        for w, d in enumerate(DILATIONS):
            single = seq // d == ATT_TQ
            variants = {3: (ATT_HALF, True, True)} if single else {
                0: (ATT_HALF, False, False), 1: (ATT_HALF, True, False), 2: (ATT_HALF, False, True)}
            variants[4] = (0, True, single)
            variants[5] = (2 * ATT_HALF, single, True)
            for variant, (shift, first, last) in variants.items():
                rel = jnp.abs(qi + shift - kc)
                valid = rel <= ATT_HALF
                if first:
                    valid = valid & (kc >= shift)
                if last:
                    valid = valid & (kc < shift + ATT_TQ)
                dist = (d * rel).astype(_F32)
                for j in range(2):
                    bias_ref[w, j, variant] = jnp.where(valid, (-slope(j) * dist) * LOG2E, NEG_INF)

    for w, d in enumerate(DILATIONS):
        q_ref, k_ref, v_ref = qkv_refs[w]
        sub_len = seq // d
        n_blocks = sub_len // ATT_TQ
        n_total = d * n_blocks

        def coords(blk):
            r = blk // n_blocks
            b = blk - r * n_blocks
            if isinstance(blk, int):
                at_start, at_end = int(blk == 0), int(blk == n_total - 1)
                variant = 4 if at_start else 5 if at_end else int(b == 0) + 2 * int(b == n_blocks - 1)
            else:
                at_start, at_end = jnp.where(blk == 0, 1, 0), jnp.where(blk == n_total - 1, 1, 0)
                edge = jnp.where(b == 0, 1, 0) + jnp.where(b == n_blocks - 1, 2, 0)
                variant = jnp.where(blk == 0, 4, jnp.where(blk == n_total - 1, 5, edge))
            q0 = _aligned(blk * ATT_TQ, ATT_TQ)
            t0 = _aligned(blk * ATT_TQ - ATT_HALF + ATT_HALF * (at_start - at_end), ATT_HALF)
            return r, b, q0, t0, variant

        def score(grp, slot):
            for u in range(group):
                _, _, q0, t0, variant = coords(grp * group + u)
                q2 = jnp.concatenate(heads(q_ref[0, 0,pl.ds(q0, ATT_TQ), :]), axis=0)
                s = lax.dot_general(q2, k_ref[0, 0,pl.ds(t0, ATT_TK), :], _NT, preferred_element_type=_F32)
                for j in range(2):
                    s_buf[slot, u, j] = s[j * ATT_TQ:(j + 1) * ATT_TQ] + bias_ref[w, j, variant]

        def softmax(slot):
            for u in range(group):
                tops = [jnp.max(s_buf[slot, u, j], axis=-1, keepdims=True) for j in range(2)]
                m_buf[slot, u] = jnp.where(head_a, tops[0], tops[1])
                for j in range(2):
                    p_buf[slot, u, j] = jnp.exp2(s_buf[slot, u, j] - tops[j]).astype(_BF)

        def output(grp, slot):
            for u in range(group):
                r, b, q0, t0, _ = coords(grp * group + u)
                v_a, v_b = heads(v_ref[0, 0,pl.ds(t0, ATT_TK), :])
                acc = jnp.dot(p_buf[slot, u, 0], jnp.concatenate([v_a, mask_a], axis=1),
                              preferred_element_type=_F32)
                acc = acc + jnp.dot(p_buf[slot, u, 1], jnp.concatenate([v_b, mask_b], axis=1),
                                    preferred_element_type=_F32)
                den = acc[:, LANES:]
                dst = pl.ds(r + d * b * ATT_TQ, ATT_TQ, stride=d) if d > 1 else pl.ds(q0, ATT_TQ)
                od[w, dst, :] = acc[:, :LANES] / den
                ld[w, dst, :] = m_buf[slot, u] + jnp.log2(den)

        n_groups = ATT_GROUPS

        score(0, 0)
        softmax(0)
        score(1, 1)

        def steady(t, carry):
            for slot in range(2):
                g = 2 * t + slot
                output(g, slot)
                softmax(1 - slot)
                score(g + 2, slot)
            return carry

        lax.fori_loop(0, n_groups // 2 - 1, steady, 0)
        output(n_groups - 2, 0)
        softmax(1)
        output(n_groups - 1, 1)

    def merge(it, carry):
        for u in range(MERGE_UNROLL):
            rows = pl.ds(pl.multiple_of((it * MERGE_UNROLL + u) * ATT_TQ, ATT_TQ), ATT_TQ)
            lses = [ld[w, rows, :] for w in range(len(DILATIONS))]
            top = jnp.maximum(jnp.maximum(lses[0], lses[1]), lses[2])
            wts = [jnp.exp2(l - top) for l in lses]
            num = wts[0] * od[0, rows, :] + wts[1] * od[1, rows, :] + wts[2] * od[2, rows, :]
            o_ref[0, 0, rows, :] = (num / (wts[0] + wts[1] + wts[2])).astype(o_ref.dtype)
        return carry

    lax.fori_loop(0, seq // ATT_TQ // MERGE_UNROLL, merge, 0)


def _attention(qkv_by_dilation, batch, seq):
    n_pairs = ATT_W // LANES
    group = seq // ATT_TQ // ATT_GROUPS
    blk = (1, 1, seq, LANES)
    part = lambda j: pl.BlockSpec(blk, lambda p, b: (b, j * n_pairs + p, 0, 0))
    return pl.pallas_call(
        functools.partial(_att_kernel, seq=seq),
        out_shape=jax.ShapeDtypeStruct((batch, n_pairs, seq, LANES), _BF),
        grid=(n_pairs, batch),
        in_specs=[part(j) for _ in DILATIONS for j in range(3)],
        out_specs=part(0),
        scratch_shapes=[pltpu.VMEM((len(DILATIONS), seq, LANES), _F32)] * 2
        + [pltpu.VMEM((len(DILATIONS), 2, ATT_VARIANTS, ATT_TQ, ATT_TK), _F32),
           pltpu.VMEM((ATT_SLOTS, group, 2, ATT_TQ, ATT_TK), _F32),
           pltpu.VMEM((ATT_SLOTS, group, 2, ATT_TQ, ATT_TK), _BF),
           pltpu.VMEM((ATT_SLOTS, group, ATT_TQ, LANES), _F32)],
        compiler_params=_params(2),
        name="attention",
    )(*[a for a in qkv_by_dilation for _ in range(3)])


RET_GROUPS = 2


def _ret_kernel(logit_ref, q_ref, k_ref, v_ref, g_ref, o_ref, sf, sb, lhs_buf, mix_buf, *, seq):
    head = pl.program_id(1)
    c = RET_CHUNK
    n_chunks = seq // c
    group = n_chunks // RET_GROUPS
    scale = RET_HEAD_DIM ** -0.5

    def log_sigmoid(x):
        return jnp.minimum(x, 0.0) - jnp.log(1.0 + jnp.exp(-jnp.abs(x)))

    lg_f = log_sigmoid(jnp.full((c, LANES), logit_ref[0, head], _F32))
    lg_b = log_sigmoid(jnp.full((c, LANES), logit_ref[1, head], _F32))
    pos = lax.broadcasted_iota(jnp.int32, (c, LANES), 0).astype(_F32)
    col = lax.broadcasted_iota(jnp.int32, (c, LANES), 1).astype(_F32)
    kw_f = jnp.exp(lg_f * (c - 1.0 - pos)) * scale
    kw_b = jnp.exp(lg_b * pos) * scale
    qw_f = jnp.exp(lg_f * (pos + 1.0))
    qw_b = jnp.exp(lg_b * (c - pos))
    g_f = jnp.exp(lg_f * c)
    g_b = jnp.exp(lg_b * c)
    rel = pos - col
    decay = jnp.where(rel >= 0, jnp.exp(lg_f * jnp.maximum(rel, 0.0)),
                      jnp.exp(lg_b * jnp.maximum(-rel, 0.0))) * scale
    lane = lax.broadcasted_iota(jnp.int32, (1, LANES), 1)
    real = lane < RET_HEAD_DIM

    for i in range(n_chunks):
        k = k_ref[0, 0, i * c:(i + 1) * c, :].astype(_F32)
        v = v_ref[0, 0, i * c:(i + 1) * c, :]
        sf[i] = lax.dot_general((k * kw_f).astype(_BF), v, _TN, preferred_element_type=_F32)
        sb[i] = lax.dot_general((k * kw_b).astype(_BF), v, _TN, preferred_element_type=_F32)

    def scan(j, states):
        fwd, bwd = states
        i = n_chunks - 1 - j
        inc_f, inc_b = sf[j], sb[i]
        sf[j] = fwd
        sb[i] = bwd
        return fwd * g_f + inc_f, bwd * g_b + inc_b

    zero_state = jnp.zeros((LANES, LANES), _F32)
    lax.fori_loop(0, n_chunks, scan, (zero_state, zero_state))

    def chunk_rows(grp, u):
        i = grp * group + u
        return i, pl.ds(_aligned(i * c, c), c)

    def weigh(grp, slot):
        for u in range(group):
            _, rows = chunk_rows(grp, u)
            q = q_ref[0, 0,rows, :]
            qf = q.astype(_F32)
            scores = lax.dot_general(q, k_ref[0, 0,rows, :], _NT, preferred_element_type=_F32) * decay
            lhs_buf[slot, u, 0] = scores.astype(_BF)
            lhs_buf[slot, u, 1] = (qf * qw_f).astype(_BF)
            lhs_buf[slot, u, 2] = (qf * qw_b).astype(_BF)

    def mix(grp, slot):
        for u in range(group):
            i, rows = chunk_rows(grp, u)
            o = jnp.dot(lhs_buf[slot, u, 0], v_ref[0, 0,rows, :], preferred_element_type=_F32)
            o = o + jnp.dot(lhs_buf[slot, u, 1], sf[i].astype(_BF), preferred_element_type=_F32)
            o = o + jnp.dot(lhs_buf[slot, u, 2], sb[i].astype(_BF), preferred_element_type=_F32)
            mix_buf[slot, u] = o

    def finish(grp, slot):
        for u in range(group):
            _, rows = chunk_rows(grp, u)
            o = mix_buf[slot, u]
            mu = jnp.sum(o, axis=-1, keepdims=True) * (1.0 / RET_HEAD_DIM)
            dev = jnp.where(real, o - mu, 0.0)
            var = jnp.sum(dev * dev, axis=-1, keepdims=True) * (1.0 / RET_HEAD_DIM)
            g = g_ref[0, 0, rows, :]
            o_ref[0, 0, rows, :] = (dev * lax.rsqrt(var + NORM_EPS) * (g / (1.0 + jnp.exp(-g)))).astype(o_ref.dtype)

    n_groups = RET_GROUPS
    weigh(0, 0)
    mix(0, 0)
    weigh(1, 1)

    def steady(t, carry):
        for slot in range(2):
            grp = 2 * t + slot
            finish(grp, slot)
            mix(grp + 1, 1 - slot)
            weigh(grp + 2, slot)
        return carry

    lax.fori_loop(0, n_groups // 2 - 1, steady, 0)
    finish(n_groups - 2, 0)
    mix(n_groups - 1, 1)
    finish(n_groups - 1, 1)


def _retention(qkv, gate, logits, batch, seq):
    blk = (1, 1, seq, LANES)
    n_chunks = seq // RET_CHUNK
    group = n_chunks // RET_GROUPS
    return pl.pallas_call(
        functools.partial(_ret_kernel, seq=seq),
        out_shape=jax.ShapeDtypeStruct((batch, RET_HEADS, seq, LANES), _BF),
        grid=(batch, RET_HEADS),
        in_specs=[pl.BlockSpec(memory_space=pltpu.SMEM),
                  pl.BlockSpec(blk, lambda b, h: (b, 3 * h, 0, 0)),
                  pl.BlockSpec(blk, lambda b, h: (b, 3 * h + 1, 0, 0)),
                  pl.BlockSpec(blk, lambda b, h: (b, 3 * h + 2, 0, 0)),
                  pl.BlockSpec(blk, lambda b, h: (b, h, 0, 0))],
        out_specs=pl.BlockSpec(blk, lambda b, h: (b, h, 0, 0)),
        scratch_shapes=[pltpu.VMEM((n_chunks, LANES, LANES), _F32)] * 2
        + [pltpu.VMEM((2, group, 3, RET_CHUNK, LANES), _BF),
           pltpu.VMEM((2, group, RET_CHUNK, LANES), _F32)],
        compiler_params=_params(2),
        name="retention",
    )(logits, qkv, qkv, qkv, gate)


def _mix_out_ffn_kernel(att_ref, ret_ref, b_ref, u_ref, up_ref, un_ref, cw_ref, wa_ref, wr_ref, wc_ref,
                        mix_gain_ref, x_ref, ffn_gain_ref, wg_ref, wu_ref, wd_ref, o_ref, *, tiles_per_seq):
    tm = u_ref.shape[0]
    t = pl.program_id(0) % tiles_per_seq
    u = u_ref[...]
    prev_row = jnp.where(t == 0, 0.0, up_ref[7:8, :])
    next_row = jnp.where(t == tiles_per_seq - 1, 0.0, un_ref[0:1, :])
    row_id = lax.broadcasted_iota(jnp.int32, u.shape, 0)
    u_prev = jnp.where(row_id == 0, prev_row, pltpu.roll(u, 1, 0))
    u_next = jnp.where(row_id == tm - 1, next_row, pltpu.roll(u, tm - 1, 0))
    conv = cw_ref[0:1, :] * u_prev + cw_ref[1:2, :] * u + cw_ref[2:3, :] * u_next
    cv = (b_ref[...] * conv).astype(_BF)
    def wide(ref, rows):
        return jnp.concatenate([ref[0, s, rows, :] for s in range(ref.shape[1])], axis=1)

    for rows in _subtiles(tm):
        y = jnp.dot(wide(att_ref, rows), wa_ref[...], preferred_element_type=_F32)
        y = y + jnp.dot(wide(ret_ref, rows), wr_ref[...], preferred_element_type=_F32)
        y = y + jnp.dot(cv[rows], wc_ref[...], preferred_element_type=_F32)
        x_mid = x_ref[rows, :] + _rms(y, mix_gain_ref[...])
        o_ref[rows, :] = _swiglu_residual(x_mid, ffn_gain_ref, wg_ref, wu_ref, wd_ref)


def _mix_out_ffn(att, ret, b, u, conv_w, wa, wr, wc, mix_gain, x, ffn_gains, wg, wu, wd, seq):
    n = x.shape[0]
    tm = FFN_SUBTILES * FFN_SUBTILE
    sub = 8
    row = lambda i: (i, 0)
    prev_blk = lambda i: (jnp.maximum(i * (tm // sub) - 1, 0), 0)
    next_blk = lambda i: (jnp.minimum((i + 1) * (tm // sub), n // sub - 1), 0)
    tiles_per_seq = seq // tm
    slab_rows = lambda i: (i // tiles_per_seq, 0, i % tiles_per_seq, 0)
    return pl.pallas_call(
        functools.partial(_mix_out_ffn_kernel, tiles_per_seq=tiles_per_seq),
        out_shape=jax.ShapeDtypeStruct(x.shape, x.dtype),
        grid=(n // tm,),
        in_specs=[pl.BlockSpec((1, ATT_W // LANES, tm, LANES), slab_rows),
                  pl.BlockSpec((1, RET_HEADS, tm, LANES), slab_rows),
                  pl.BlockSpec((tm, CONV_W), row), pl.BlockSpec((tm, CONV_W), row),
                  pl.BlockSpec((sub, CONV_W), prev_blk), pl.BlockSpec((sub, CONV_W), next_blk),
                  _resident((3, CONV_W)), _resident((ATT_W, D_MODEL)), _resident((RET_PAD_W, D_MODEL)),
                  _resident((CONV_W, D_MODEL)), _resident((1, D_MODEL)),
                  pl.BlockSpec((tm, D_MODEL), row), _resident((2, D_MODEL)),
                  _resident((D_MODEL, D_FF)), _resident((D_MODEL, D_FF)), _resident((D_FF, D_MODEL))],
        out_specs=pl.BlockSpec((tm, D_MODEL), row),
        compiler_params=_params(1),
        name="mix_out_ffn",
    )(att, ret, b, u, u, u, conv_w, wa, wr, wc, mix_gain, x, ffn_gains, wg, wu, wd)


def _pad_heads(w, axis):
    shape = w.shape[:axis] + (RET_HEADS, RET_HEAD_DIM) + w.shape[axis + 1:]
    pad = [(0, 0)] * (len(shape))
    pad[axis + 1] = (0, LANES - RET_HEAD_DIM)
    w = jnp.pad(w.reshape(shape), pad)
    return w.reshape(w.shape[:axis] + (RET_PAD_W,) + w.shape[axis + 2:])


def _mix_in_weights(w):
    att = w[:, :_IN_ATT]
    c0 = _IN_ATT
    parts = [_pad_heads(w[:, c0 + j * RET_W:c0 + (j + 1) * RET_W], 1).reshape(D_MODEL, RET_HEADS, LANES)
             for j in range(4)]
    qkv = jnp.stack(parts[:3], axis=2).reshape(D_MODEL, _IN_RET)
    gate = parts[3].reshape(D_MODEL, _IN_GATE)
    conv = w[:, c0 + 4 * RET_W:]
    return jnp.concatenate([att, qkv, gate, conv], axis=1).astype(_BF)


def _trunk(x, batch, seq, weights):
    for layer in weights:
        x = _ffn(x, layer["gain"][0:2], *layer["ffn1"])
        att_qkv, ret_qkv, gate, b, u = _inproj(x, layer["gain"][2:3], layer["w_in"], batch, seq)
        att = _attention(att_qkv, batch, seq)
        ret = _retention(ret_qkv, gate, layer["logit"], batch, seq)
        x = _mix_out_ffn(att, ret, b, u, layer["conv_w"], *layer["w_out"], layer["gain"][3:4], x,
                         layer["gain"][4:6], *layer["ffn2"], seq)
    return x


def kernel(x_prompt, x_sample, norm_gain, ffn1_w_gate, ffn1_w_up, ffn1_w_down, w_mix_in, conv_w,
           ret_decay_logit, w_mix_out, ffn2_w_gate, ffn2_w_up, ffn2_w_down):
    weights = []
    for l in range(DEPTH):
        wo = w_mix_out[l]
        weights.append(dict(
            gain=norm_gain[l],
            ffn1=tuple(w[l].astype(_BF) for w in (ffn1_w_gate, ffn1_w_up, ffn1_w_down)),
            ffn2=tuple(w[l].astype(_BF) for w in (ffn2_w_gate, ffn2_w_up, ffn2_w_down)),
            w_in=_mix_in_weights(w_mix_in[l]),
            conv_w=conv_w[l],
            logit=ret_decay_logit[l],
            w_out=(wo[:ATT_W].astype(_BF), _pad_heads(wo[ATT_W:ATT_W + RET_W], 0).astype(_BF),
                   wo[ATT_W + RET_W:].astype(_BF)),
        ))
    outs = []
    for x in (x_prompt, x_sample):
        batch, seq, _ = x.shape
        outs.append(_trunk(x.reshape(batch * seq, D_MODEL), batch, seq, weights).reshape(x.shape))
    return tuple(outs)
```

```python
import functools
import math

import jax
import jax.numpy as jnp
import numpy as np
from jax import lax
from jax.experimental import pallas as pl
from jax.experimental.pallas import tpu as pltpu

D_MODEL = 1024
D_FF = 2816
DEPTH = 2
HEAD_DIM = 64
ATT_HEADS = 6
ATT_W = ATT_HEADS * HEAD_DIM
DILATIONS = (1, 4, 16)
ATT_HALF = 64
RET_HEADS = 4
RET_HEAD_DIM = 96
RET_W = RET_HEADS * RET_HEAD_DIM
RET_CHUNK = 128
CONV_W = 256
NORM_EPS = 1e-6
NEG_INF = -1e30
LOG2E = math.log2(math.e)
ATT_Q_SCALE = HEAD_DIM ** -0.5 * LOG2E

LANES = 128
RET_PAD_W = RET_HEADS * LANES
ATT_TQ = 128
ATT_TK = ATT_TQ + 2 * ATT_HALF
FF_CHUNK = 256
CAST_ROWS = 256
FFN_SUBTILE = 512
FFN_SUBTILES = 2
VMEM_LIMIT = 56 * 1024 * 1024

_BF = jnp.bfloat16
_F32 = jnp.float32
_NT = (((1,), (1,)), ((), ()))
_TN = (((0,), (0,)), ((), ()))


def _alibi_slope_list(n):
    def pow2(m):
        start = 2.0 ** (-8.0 / m)
        return [start ** (i + 1) for i in range(m)]
    if math.log2(n).is_integer():
        return pow2(n)
    c = 2 ** math.floor(math.log2(n))
    return pow2(c) + _alibi_slope_list(2 * c)[0::2][: n - c]


_SLOPES = [float(np.float32(s)) for s in _alibi_slope_list(ATT_HEADS)]


def _aligned(x, m):
    return x if isinstance(x, int) else pl.multiple_of(x, m)


def _rms(x, g):
    return x * lax.rsqrt(jnp.mean(x * x, axis=-1, keepdims=True) + NORM_EPS) * g


def _params(n_axes):
    return pltpu.CompilerParams(dimension_semantics=("arbitrary",) * n_axes,
                                vmem_limit_bytes=VMEM_LIMIT)


def _resident(shape):
    return pl.BlockSpec(shape, lambda *_: (0,) * len(shape), pipeline_mode=pl.Buffered(1))


def _swiglu_residual(x, gain_ref, wg_ref, wu_ref, wd_ref):
    h = _rms(x, gain_ref[0:1, :]).astype(_BF)
    acc = jnp.zeros(x.shape, _F32)
    for c in range(D_FF // FF_CHUNK):
        sl = slice(c * FF_CHUNK, (c + 1) * FF_CHUNK)
        g = jnp.dot(h, wg_ref[:, sl], preferred_element_type=_F32)
        u = jnp.dot(h, wu_ref[:, sl], preferred_element_type=_F32)
        a = (g / (1.0 + jnp.exp(-g)) * u).astype(_BF)
        acc = acc + jnp.dot(a, wd_ref[sl, :], preferred_element_type=_F32)
    return x + 0.5 * _rms(acc, gain_ref[1:2, :])


def _subtiles(n_rows):
    return [slice(r, r + FFN_SUBTILE) for r in range(0, n_rows, FFN_SUBTILE)]


def _ffn_kernel(x_ref, gain_ref, wg_ref, wu_ref, wd_ref, o_ref):
    for rows in _subtiles(x_ref.shape[0]):
        o_ref[rows, :] = _swiglu_residual(x_ref[rows, :], gain_ref, wg_ref, wu_ref, wd_ref)


def _ffn(x, gains, wg, wu, wd):
    n = x.shape[0]
    tm = FFN_SUBTILES * FFN_SUBTILE
    row = lambda i: (i, 0)
    return pl.pallas_call(
        _ffn_kernel,
        out_shape=jax.ShapeDtypeStruct(x.shape, x.dtype),
        grid=(n // tm,),
        in_specs=[pl.BlockSpec((tm, D_MODEL), row), _resident((2, D_MODEL)),
                  _resident((D_MODEL, D_FF)), _resident((D_MODEL, D_FF)), _resident((D_FF, D_MODEL))],
        out_specs=pl.BlockSpec((tm, D_MODEL), row),
        compiler_params=_params(1),
        name="ffn",
    )(x, gains, wg, wu, wd)


_IN_ATT = 3 * ATT_W
_IN_RET = RET_HEADS * 3 * LANES
_IN_GATE = RET_PAD_W
_IN_CONV = 3 * CONV_W
_IN_TOTAL = _IN_ATT + _IN_RET + _IN_GATE + _IN_CONV
IN_CHUNK = 256


def _inproj_kernel(x_ref, gain_ref, w_ref, att1_ref, att4_ref, att16_ref, ret_ref, gate_ref, b_ref, u_ref,
                   stage, stage4):
    for t, rows in enumerate(_subtiles(x_ref.shape[0])):
        _inproj_subtile(t, rows, x_ref, gain_ref, w_ref, att1_ref, att4_ref, att16_ref, ret_ref, gate_ref, b_ref,
                        u_ref, stage, stage4)


def _inproj_subtile(t, rows, x_ref, gain_ref, w_ref, att1_ref, att4_ref, att16_ref, ret_ref, gate_ref, b_ref,
                    u_ref, stage, stage4):
    tm = FFN_SUBTILE
    h = _rms(x_ref[rows, :], gain_ref[...]).astype(_BF)
    c0 = 0
    def project(start, width):
        for c in range(0, width, IN_CHUNK):
            cw = min(IN_CHUNK, width - c)
            yield c, jnp.dot(h, w_ref[:, start + c:start + c + cw], preferred_element_type=_F32)

    n_slabs = _IN_ATT // LANES
    for c, z in project(c0, _IN_ATT):
        for s in range(c // LANES, (c + z.shape[1]) // LANES):
            zs = z[:, s * LANES - c:(s + 1) * LANES - c]
            if (s + 1) * LANES <= ATT_W:
                zs = zs * ATT_Q_SCALE
            stage[s] = zs
            att1_ref[0, s, rows, :] = zs.astype(_BF)
    d4, d16 = DILATIONS[1], DILATIONS[2]
    rows4, rows16 = tm // d4, tm // d16
    for s in range(n_slabs):
        for r4 in range(d4):
            x = stage[s, pl.ds(r4, rows4, stride=d4), :]
            att4_ref[0, s, r4, t * rows4:(t + 1) * rows4, :] = x.astype(_BF)
            stage4[s, r4 * rows4:(r4 + 1) * rows4, :] = x
        for r4 in range(d4):
            for j in range(d16 // d4):
                x = stage4[s, pl.ds(r4 * rows4 + j, rows16, stride=d16 // d4), :]
                att16_ref[0, s, r4 + d4 * j, t * rows16:(t + 1) * rows16, :] = x.astype(_BF)
    c0 += _IN_ATT

    def slabs(ref, c, z, dtype):
        for s in range(z.shape[1] // LANES):
            ref[0, c // LANES + s, rows, :] = z[:, s * LANES:(s + 1) * LANES].astype(dtype)

    for c, z in project(c0, _IN_RET):
        slabs(ret_ref, c, z, _BF)
    c0 += _IN_RET
    for c, z in project(c0, _IN_GATE):
        slabs(gate_ref, c, z, _F32)
    c0 += _IN_GATE
    (_, zb), (_, zc), (_, zu) = project(c0, _IN_CONV)
    b_ref[rows, :] = zb
    u_ref[rows, :] = zc * zu


def _inproj(x, gain, w, batch, seq):
    n = x.shape[0]
    tm = FFN_SUBTILES * FFN_SUBTILE
    tiles_per_seq = seq // tm
    row = lambda i: (i, 0)
    where = lambda i: (i // tiles_per_seq, i % tiles_per_seq)

    def slab_out(width, dtype, d=1):
        n_slabs = width // LANES
        if d == 1:
            return (jax.ShapeDtypeStruct((batch, n_slabs, seq, LANES), dtype),
                    pl.BlockSpec((1, n_slabs, tm, LANES), lambda i: (where(i)[0], 0, where(i)[1], 0)))
        return (jax.ShapeDtypeStruct((batch, n_slabs, d, seq // d, LANES), dtype),
                pl.BlockSpec((1, n_slabs, d, tm // d, LANES), lambda i: (where(i)[0], 0, 0, where(i)[1], 0)))

    flat = [(jax.ShapeDtypeStruct((n, CONV_W), _F32), pl.BlockSpec((tm, CONV_W), row))] * 2
    outs = [slab_out(_IN_ATT, _BF, d) for d in DILATIONS] + [slab_out(_IN_RET, _BF), slab_out(_IN_GATE, _F32)] + flat
    att1, att4, att16, ret, gate, b, u = pl.pallas_call(
        _inproj_kernel,
        out_shape=[o[0] for o in outs],
        grid=(n // tm,),
        in_specs=[pl.BlockSpec((tm, D_MODEL), row), _resident((1, D_MODEL)),
                  _resident((D_MODEL, _IN_TOTAL))],
        out_specs=[o[1] for o in outs],
        scratch_shapes=[pltpu.VMEM((_IN_ATT // LANES, FFN_SUBTILE, LANES), _F32)] * 2,
        compiler_params=_params(1),
        name="inproj",
    )(x, gain, w)
    att = [a.reshape(batch, _IN_ATT // LANES, seq, LANES) for a in (att1, att4, att16)]
    return att, ret, gate, b, u


ATT_GROUPS = 4
MERGE_UNROLL = 8
ATT_SLOTS = 2
ATT_VARIANTS = 6


def _att_kernel(*refs, seq):
    n_win = len(DILATIONS)
    group = seq // ATT_TQ // ATT_GROUPS
    qkv_refs = [refs[3 * w:3 * w + 3] for w in range(n_win)]
    o_ref, od, ld, bias_ref, s_buf, p_buf, m_buf = refs[3 * n_win:]
    pair = pl.program_id(0)
    lane = lax.broadcasted_iota(jnp.int32, (1, LANES), 1)
    head_a = lane < HEAD_DIM
    zero_bf = jnp.zeros((), _BF)

    mask_a = jnp.broadcast_to(jnp.where(head_a, 1.0, 0.0), (ATT_TK, LANES)).astype(_BF)
    mask_b = jnp.broadcast_to(jnp.where(head_a, 0.0, 1.0), (ATT_TK, LANES)).astype(_BF)

    def heads(x):
        return jnp.where(head_a, x, zero_bf), jnp.where(head_a, zero_bf, x)

    def slope(j):
        s = jnp.float32(_SLOPES[j])
        for p in range(1, ATT_HEADS // 2):
            s = jnp.where(pair == p, jnp.float32(_SLOPES[2 * p + j]), s)
        return s

    @pl.when(pl.program_id(1) == 0)
    def _():
        qi = lax.broadcasted_iota(jnp.int32, (ATT_TQ, ATT_TK), 0)
        kc = lax.broadcasted_iota(jnp.int32, (ATT_TQ, ATT_TK), 1)
        for w, d in enumerate(DILATIONS):
            single = seq // d == ATT_TQ
            variants = {3: (ATT_HALF, True, True)} if single else {
                0: (ATT_HALF, False, False), 1: (ATT_HALF, True, False), 2: (ATT_HALF, False, True)}
            variants[4] = (0, True, single)
            variants[5] = (2 * ATT_HALF, single, True)
            for variant, (shift, first, last) in variants.items():
                rel = jnp.abs(qi + shift - kc)
                valid = rel <= ATT_HALF
                if first:
                    valid = valid & (kc >= shift)
                if last:
                    valid = valid & (kc < shift + ATT_TQ)
                dist = (d * rel).astype(_F32)
                for j in range(2):
                    bias_ref[w, j, variant] = jnp.where(valid, (-slope(j) * dist) * LOG2E, NEG_INF)

    for w, d in enumerate(DILATIONS):
        q_ref, k_ref, v_ref = qkv_refs[w]
        sub_len = seq // d
        n_blocks = sub_len // ATT_TQ
        n_total = d * n_blocks

        def coords(blk):
            r = blk // n_blocks
            b = blk - r * n_blocks
            if isinstance(blk, int):
                at_start, at_end = int(blk == 0), int(blk == n_total - 1)
                variant = 4 if at_start else 5 if at_end else int(b == 0) + 2 * int(b == n_blocks - 1)
            else:
                at_start, at_end = jnp.where(blk == 0, 1, 0), jnp.where(blk == n_total - 1, 1, 0)
                edge = jnp.where(b == 0, 1, 0) + jnp.where(b == n_blocks - 1, 2, 0)
                variant = jnp.where(blk == 0, 4, jnp.where(blk == n_total - 1, 5, edge))
            q0 = _aligned(blk * ATT_TQ, ATT_TQ)
            t0 = _aligned(blk * ATT_TQ - ATT_HALF + ATT_HALF * (at_start - at_end), ATT_HALF)
            return r, b, q0, t0, variant

        def score(grp, slot):
            for u in range(group):
                _, _, q0, t0, variant = coords(grp * group + u)
                q2 = jnp.concatenate(heads(q_ref[0, 0,pl.ds(q0, ATT_TQ), :]), axis=0)
                s = lax.dot_general(q2, k_ref[0, 0,pl.ds(t0, ATT_TK), :], _NT, preferred_element_type=_F32)
                for j in range(2):
                    s_buf[slot, u, j] = s[j * ATT_TQ:(j + 1) * ATT_TQ] + bias_ref[w, j, variant]

        def softmax(slot):
            for u in range(group):
                tops = [jnp.max(s_buf[slot, u, j], axis=-1, keepdims=True) for j in range(2)]
                m_buf[slot, u] = jnp.where(head_a, tops[0], tops[1])
                for j in range(2):
                    p_buf[slot, u, j] = jnp.exp2(s_buf[slot, u, j] - tops[j]).astype(_BF)

        def output(grp, slot):
            for u in range(group):
                r, b, q0, t0, _ = coords(grp * group + u)
                v_a, v_b = heads(v_ref[0, 0,pl.ds(t0, ATT_TK), :])
                acc = jnp.dot(p_buf[slot, u, 0], jnp.concatenate([v_a, mask_a], axis=1),
                              preferred_element_type=_F32)
                acc = acc + jnp.dot(p_buf[slot, u, 1], jnp.concatenate([v_b, mask_b], axis=1),
                                    preferred_element_type=_F32)
                den = acc[:, LANES:]
                dst = pl.ds(r + d * b * ATT_TQ, ATT_TQ, stride=d) if d > 1 else pl.ds(q0, ATT_TQ)
                od[w, dst, :] = acc[:, :LANES] / den
                ld[w, dst, :] = m_buf[slot, u] + jnp.log2(den)

        n_groups = ATT_GROUPS

        score(0, 0)
        softmax(0)
        score(1, 1)

        def steady(t, carry):
            for slot in range(2):
                g = 2 * t + slot
                output(g, slot)
                softmax(1 - slot)
                score(g + 2, slot)
            return carry

        lax.fori_loop(0, n_groups // 2 - 1, steady, 0)
        output(n_groups - 2, 0)
        softmax(1)
        output(n_groups - 1, 1)

    def merge(it, carry):
        for u in range(MERGE_UNROLL):
            rows = pl.ds(pl.multiple_of((it * MERGE_UNROLL + u) * ATT_TQ, ATT_TQ), ATT_TQ)
            lses = [ld[w, rows, :] for w in range(len(DILATIONS))]
            top = jnp.maximum(jnp.maximum(lses[0], lses[1]), lses[2])
            wts = [jnp.exp2(l - top) for l in lses]
            num = wts[0] * od[0, rows, :] + wts[1] * od[1, rows, :] + wts[2] * od[2, rows, :]
            o_ref[0, 0, rows, :] = (num / (wts[0] + wts[1] + wts[2])).astype(o_ref.dtype)
        return carry

    lax.fori_loop(0, seq // ATT_TQ // MERGE_UNROLL, merge, 0)


def _attention(qkv_by_dilation, batch, seq):
    n_pairs = ATT_W // LANES
    group = seq // ATT_TQ // ATT_GROUPS
    blk = (1, 1, seq, LANES)
    part = lambda j: pl.BlockSpec(blk, lambda p, b: (b, j * n_pairs + p, 0, 0))
    return pl.pallas_call(
        functools.partial(_att_kernel, seq=seq),
        out_shape=jax.ShapeDtypeStruct((batch, n_pairs, seq, LANES), _BF),
        grid=(n_pairs, batch),
        in_specs=[part(j) for _ in DILATIONS for j in range(3)],
        out_specs=part(0),
        scratch_shapes=[pltpu.VMEM((len(DILATIONS), seq, LANES), _F32)] * 2
        + [pltpu.VMEM((len(DILATIONS), 2, ATT_VARIANTS, ATT_TQ, ATT_TK), _F32),
           pltpu.VMEM((ATT_SLOTS, group, 2, ATT_TQ, ATT_TK), _F32),
           pltpu.VMEM((ATT_SLOTS, group, 2, ATT_TQ, ATT_TK), _BF),
           pltpu.VMEM((ATT_SLOTS, group, ATT_TQ, LANES), _F32)],
        compiler_params=_params(2),
        name="attention",
    )(*[a for a in qkv_by_dilation for _ in range(3)])


RET_HEADS_PER_STEP = 2
RET_GROUPS = 2


def _ret_kernel(logit_ref, qkv_ref, gate_ref, out_ref, *scratch, seq):
    per_head = len(scratch) // RET_HEADS_PER_STEP
    for j in range(RET_HEADS_PER_STEP):
        _ret_head(pl.program_id(1) * RET_HEADS_PER_STEP + j, logit_ref,
                  *(qkv_ref.at[:, 3 * j + i:3 * j + i + 1] for i in range(3)),
                  gate_ref.at[:, j:j + 1], out_ref.at[:, j:j + 1],
                  *scratch[j * per_head:(j + 1) * per_head], seq=seq)


def _ret_head(head, logit_ref, q_ref, k_ref, v_ref, g_ref, o_ref, sf, sb, lhs_buf, mix_buf, *, seq):
    c = RET_CHUNK
    n_chunks = seq // c
    group = n_chunks // RET_GROUPS
    scale = RET_HEAD_DIM ** -0.5

    def log_sigmoid(x):
        return jnp.minimum(x, 0.0) - jnp.log(1.0 + jnp.exp(-jnp.abs(x)))

    lg_f = log_sigmoid(jnp.full((c, LANES), logit_ref[0, head], _F32))
    lg_b = log_sigmoid(jnp.full((c, LANES), logit_ref[1, head], _F32))
    pos = lax.broadcasted_iota(jnp.int32, (c, LANES), 0).astype(_F32)
    col = lax.broadcasted_iota(jnp.int32, (c, LANES), 1).astype(_F32)
    kw_f = jnp.exp(lg_f * (c - 1.0 - pos)) * scale
    kw_b = jnp.exp(lg_b * pos) * scale
    qw_f = jnp.exp(lg_f * (pos + 1.0))
    qw_b = jnp.exp(lg_b * (c - pos))
    g_f = jnp.exp(lg_f * c)
    g_b = jnp.exp(lg_b * c)
    rel = pos - col
    decay = jnp.where(rel >= 0, jnp.exp(lg_f * jnp.maximum(rel, 0.0)),
                      jnp.exp(lg_b * jnp.maximum(-rel, 0.0))) * scale
    lane = lax.broadcasted_iota(jnp.int32, (1, LANES), 1)
    real = lane < RET_HEAD_DIM

    for i in range(n_chunks):
        k = k_ref[0, 0, i * c:(i + 1) * c, :].astype(_F32)
        v = v_ref[0, 0, i * c:(i + 1) * c, :]
        sf[i] = lax.dot_general((k * kw_f).astype(_BF), v, _TN, preferred_element_type=_F32)
        sb[i] = lax.dot_general((k * kw_b).astype(_BF), v, _TN, preferred_element_type=_F32)

    def scan(j, states):
        fwd, bwd = states
        i = n_chunks - 1 - j
        inc_f, inc_b = sf[j], sb[i]
        sf[j] = fwd
        sb[i] = bwd
        return fwd * g_f + inc_f, bwd * g_b + inc_b

    zero_state = jnp.zeros((LANES, LANES), _F32)
    lax.fori_loop(0, n_chunks, scan, (zero_state, zero_state))

    def chunk_rows(grp, u):
        i = grp * group + u
        return i, pl.ds(_aligned(i * c, c), c)

    def weigh(grp, slot):
        for u in range(group):
            _, rows = chunk_rows(grp, u)
            q = q_ref[0, 0,rows, :]
            qf = q.astype(_F32)
            scores = lax.dot_general(q, k_ref[0, 0,rows, :], _NT, preferred_element_type=_F32) * decay
            lhs_buf[slot, u, 0] = scores.astype(_BF)
            lhs_buf[slot, u, 1] = (qf * qw_f).astype(_BF)
            lhs_buf[slot, u, 2] = (qf * qw_b).astype(_BF)

    def mix(grp, slot):
        for u in range(group):
            i, rows = chunk_rows(grp, u)
            o = jnp.dot(lhs_buf[slot, u, 0], v_ref[0, 0,rows, :], preferred_element_type=_F32)
            o = o + jnp.dot(lhs_buf[slot, u, 1], sf[i].astype(_BF), preferred_element_type=_F32)
            o = o + jnp.dot(lhs_buf[slot, u, 2], sb[i].astype(_BF), preferred_element_type=_F32)
            mix_buf[slot, u] = o

    def finish(grp, slot):
        for u in range(group):
            _, rows = chunk_rows(grp, u)
            o = mix_buf[slot, u]
            mu = jnp.sum(o, axis=-1, keepdims=True) * (1.0 / RET_HEAD_DIM)
            dev = jnp.where(real, o - mu, 0.0)
            var = jnp.sum(dev * dev, axis=-1, keepdims=True) * (1.0 / RET_HEAD_DIM)
            g = g_ref[0, 0, rows, :]
            o_ref[0, 0, rows, :] = (dev * lax.rsqrt(var + NORM_EPS) * (g / (1.0 + jnp.exp(-g)))).astype(o_ref.dtype)

    n_groups = RET_GROUPS
    weigh(0, 0)
    mix(0, 0)
    weigh(1, 1)

    def steady(t, carry):
        for slot in range(2):
            grp = 2 * t + slot
            finish(grp, slot)
            mix(grp + 1, 1 - slot)
            weigh(grp + 2, slot)
        return carry

    lax.fori_loop(0, n_groups // 2 - 1, steady, 0)
    finish(n_groups - 2, 0)
    mix(n_groups - 1, 1)
    finish(n_groups - 1, 1)


def _retention(qkv, gate, logits, batch, seq):
    hp = RET_HEADS_PER_STEP
    n_chunks = seq // RET_CHUNK
    group = n_chunks // RET_GROUPS
    slabs = lambda n: pl.BlockSpec((1, n, seq, LANES), lambda b, p: (b, p, 0, 0))
    per_head_scratch = [pltpu.VMEM((n_chunks, LANES, LANES), _F32)] * 2 + [
        pltpu.VMEM((2, group, 3, RET_CHUNK, LANES), _BF), pltpu.VMEM((2, group, RET_CHUNK, LANES), _F32)]
    return pl.pallas_call(
        functools.partial(_ret_kernel, seq=seq),
        out_shape=jax.ShapeDtypeStruct((batch, RET_HEADS, seq, LANES), _BF),
        grid=(batch, RET_HEADS // hp),
        in_specs=[pl.BlockSpec(memory_space=pltpu.SMEM), slabs(3 * hp), slabs(hp)],
        out_specs=slabs(hp),
        scratch_shapes=per_head_scratch * hp,
        compiler_params=_params(2),
        name="retention",
    )(logits, qkv, gate)


def _mix_out_ffn_kernel(att_ref, ret_ref, b_ref, u_ref, up_ref, un_ref, cw_ref, wa_ref, wr_ref, wc_ref,
                        mix_gain_ref, x_ref, ffn_gain_ref, wg_ref, wu_ref, wd_ref, o_ref, *, tiles_per_seq):
    tm = u_ref.shape[0]
    t = pl.program_id(0) % tiles_per_seq
    u = u_ref[...]
    prev_row = jnp.where(t == 0, 0.0, up_ref[7:8, :])
    next_row = jnp.where(t == tiles_per_seq - 1, 0.0, un_ref[0:1, :])
    row_id = lax.broadcasted_iota(jnp.int32, u.shape, 0)
    u_prev = jnp.where(row_id == 0, prev_row, pltpu.roll(u, 1, 0))
    u_next = jnp.where(row_id == tm - 1, next_row, pltpu.roll(u, tm - 1, 0))
    conv = cw_ref[0:1, :] * u_prev + cw_ref[1:2, :] * u + cw_ref[2:3, :] * u_next
    cv = (b_ref[...] * conv).astype(_BF)
    def wide(ref, rows):
        return jnp.concatenate([ref[0, s, rows, :] for s in range(ref.shape[1])], axis=1)

    for rows in _subtiles(tm):
        y = jnp.dot(wide(att_ref, rows), wa_ref[...], preferred_element_type=_F32)
        y = y + jnp.dot(wide(ret_ref, rows), wr_ref[...], preferred_element_type=_F32)
        y = y + jnp.dot(cv[rows], wc_ref[...], preferred_element_type=_F32)
        x_mid = x_ref[rows, :] + _rms(y, mix_gain_ref[...])
        o_ref[rows, :] = _swiglu_residual(x_mid, ffn_gain_ref, wg_ref, wu_ref, wd_ref)


def _mix_out_ffn(att, ret, b, u, conv_w, wa, wr, wc, mix_gain, x, ffn_gains, wg, wu, wd, seq):
    n = x.shape[0]
    tm = FFN_SUBTILES * FFN_SUBTILE
    sub = 8
    row = lambda i: (i, 0)
    prev_blk = lambda i: (jnp.maximum(i * (tm // sub) - 1, 0), 0)
    next_blk = lambda i: (jnp.minimum((i + 1) * (tm // sub), n // sub - 1), 0)
    tiles_per_seq = seq // tm
    slab_rows = lambda i: (i // tiles_per_seq, 0, i % tiles_per_seq, 0)
    return pl.pallas_call(
        functools.partial(_mix_out_ffn_kernel, tiles_per_seq=tiles_per_seq),
        out_shape=jax.ShapeDtypeStruct(x.shape, x.dtype),
        grid=(n // tm,),
        in_specs=[pl.BlockSpec((1, ATT_W // LANES, tm, LANES), slab_rows),
                  pl.BlockSpec((1, RET_HEADS, tm, LANES), slab_rows),
                  pl.BlockSpec((tm, CONV_W), row), pl.BlockSpec((tm, CONV_W), row),
                  pl.BlockSpec((sub, CONV_W), prev_blk), pl.BlockSpec((sub, CONV_W), next_blk),
                  _resident((3, CONV_W)), _resident((ATT_W, D_MODEL)), _resident((RET_PAD_W, D_MODEL)),
                  _resident((CONV_W, D_MODEL)), _resident((1, D_MODEL)),
                  pl.BlockSpec((tm, D_MODEL), row), _resident((2, D_MODEL)),
                  _resident((D_MODEL, D_FF)), _resident((D_MODEL, D_FF)), _resident((D_FF, D_MODEL))],
        out_specs=pl.BlockSpec((tm, D_MODEL), row),
        compiler_params=_params(1),
        name="mix_out_ffn",
    )(att, ret, b, u, u, u, conv_w, wa, wr, wc, mix_gain, x, ffn_gains, wg, wu, wd)


def _pad_heads(w, axis):
    shape = w.shape[:axis] + (RET_HEADS, RET_HEAD_DIM) + w.shape[axis + 1:]
    pad = [(0, 0)] * (len(shape))
    pad[axis + 1] = (0, LANES - RET_HEAD_DIM)
    w = jnp.pad(w.reshape(shape), pad)
    return w.reshape(w.shape[:axis] + (RET_PAD_W,) + w.shape[axis + 2:])


def _mix_in_weights(w):
    att = w[:, :_IN_ATT]
    c0 = _IN_ATT
    parts = [_pad_heads(w[:, c0 + j * RET_W:c0 + (j + 1) * RET_W], 1).reshape(D_MODEL, RET_HEADS, LANES)
             for j in range(4)]
    qkv = jnp.stack(parts[:3], axis=2).reshape(D_MODEL, _IN_RET)
    gate = parts[3].reshape(D_MODEL, _IN_GATE)
    conv = w[:, c0 + 4 * RET_W:]
    return jnp.concatenate([att, qkv, gate, conv], axis=1).astype(_BF)


def _trunk(x, batch, seq, weights):
    for layer in weights:
        x = _ffn(x, layer["gain"][0:2], *layer["ffn1"])
        att_qkv, ret_qkv, gate, b, u = _inproj(x, layer["gain"][2:3], layer["w_in"], batch, seq)
        att = _attention(att_qkv, batch, seq)
        ret = _retention(ret_qkv, gate, layer["logit"], batch, seq)
        x = _mix_out_ffn(att, ret, b, u, layer["conv_w"], *layer["w_out"], layer["gain"][3:4], x,
                         layer["gain"][4:6], *layer["ffn2"], seq)
    return x


def _cast_kernel(w_ref, o_ref):
    o_ref[...] = w_ref[0].astype(o_ref.dtype)


def _layer_bf16(w, layer):
    _, rows, cols = w.shape
    tr = CAST_ROWS
    return pl.pallas_call(
        _cast_kernel,
        out_shape=jax.ShapeDtypeStruct((rows, cols), _BF),
        grid=(rows // tr,),
        in_specs=[pl.BlockSpec((1, tr, cols), lambda i: (layer, i, 0))],
        out_specs=pl.BlockSpec((tr, cols), lambda i: (i, 0)),
        compiler_params=_params(1),
        name="cast_bf16",
    )(w)


def kernel(x_prompt, x_sample, norm_gain, ffn1_w_gate, ffn1_w_up, ffn1_w_down, w_mix_in, conv_w,
           ret_decay_logit, w_mix_out, ffn2_w_gate, ffn2_w_up, ffn2_w_down):
    weights = []
    for l in range(DEPTH):
        wo = w_mix_out[l]
        weights.append(dict(
            gain=norm_gain[l],
            ffn1=tuple(_layer_bf16(w, l) for w in (ffn1_w_gate, ffn1_w_up, ffn1_w_down)),
            ffn2=tuple(_layer_bf16(w, l) for w in (ffn2_w_gate, ffn2_w_up, ffn2_w_down)),
            w_in=_mix_in_weights(w_mix_in[l]),
            conv_w=conv_w[l],
            logit=ret_decay_logit[l],
            w_out=(wo[:ATT_W].astype(_BF), _pad_heads(wo[ATT_W:ATT_W + RET_W], 0).astype(_BF),
                   wo[ATT_W + RET_W:].astype(_BF)),
        ))
    outs = []
    for x in (x_prompt, x_sample):
        batch, seq, _ = x.shape
        outs.append(_trunk(x.reshape(batch * seq, D_MODEL), batch, seq, weights).reshape(x.shape))
    return tuple(outs)
```

```python
import functools
import math

import jax
import jax.numpy as jnp
import numpy as np
from jax import lax
from jax.experimental import pallas as pl
from jax.experimental.pallas import tpu as pltpu

D_MODEL = 1024
D_FF = 2816
DEPTH = 2
HEAD_DIM = 64
ATT_HEADS = 6
ATT_W = ATT_HEADS * HEAD_DIM
DILATIONS = (1, 4, 16)
ATT_HALF = 64
RET_HEADS = 4
RET_HEAD_DIM = 96
RET_W = RET_HEADS * RET_HEAD_DIM
RET_CHUNK = 128
CONV_W = 256
NORM_EPS = 1e-6
NEG_INF = -1e30
LOG2E = math.log2(math.e)
ATT_Q_SCALE = HEAD_DIM ** -0.5 * LOG2E

LANES = 128
RET_PAD_W = RET_HEADS * LANES
ATT_TQ = 128
ATT_TK = ATT_TQ + 2 * ATT_HALF
FF_CHUNK = 256
CAST_ROWS = 256
FFN_SUBTILE = 512
FFN_SUBTILES = 2
VMEM_LIMIT = 56 * 1024 * 1024

_BF = jnp.bfloat16
_F32 = jnp.float32
_NT = (((1,), (1,)), ((), ()))
_TN = (((0,), (0,)), ((), ()))


def _alibi_slope_list(n):
    def pow2(m):
        start = 2.0 ** (-8.0 / m)
        return [start ** (i + 1) for i in range(m)]
    if math.log2(n).is_integer():
        return pow2(n)
    c = 2 ** math.floor(math.log2(n))
    return pow2(c) + _alibi_slope_list(2 * c)[0::2][: n - c]


_SLOPES = [float(np.float32(s)) for s in _alibi_slope_list(ATT_HEADS)]


def _aligned(x, m):
    return x if isinstance(x, int) else pl.multiple_of(x, m)


def _rms(x, g):
    return x * lax.rsqrt(jnp.mean(x * x, axis=-1, keepdims=True) + NORM_EPS) * g


def _params(n_axes):
    return pltpu.CompilerParams(dimension_semantics=("arbitrary",) * n_axes,
                                vmem_limit_bytes=VMEM_LIMIT)


def _resident(shape):
    return pl.BlockSpec(shape, lambda *_: (0,) * len(shape), pipeline_mode=pl.Buffered(1))


def _swiglu_residual(x, gain_ref, wg_ref, wu_ref, wd_ref):
    h = _rms(x, gain_ref[0:1, :]).astype(_BF)
    acc = jnp.zeros(x.shape, _F32)
    for c in range(D_FF // FF_CHUNK):
        sl = slice(c * FF_CHUNK, (c + 1) * FF_CHUNK)
        g = jnp.dot(h, wg_ref[:, sl], preferred_element_type=_F32)
        u = jnp.dot(h, wu_ref[:, sl], preferred_element_type=_F32)
        a = (g / (1.0 + jnp.exp(-g)) * u).astype(_BF)
        acc = acc + jnp.dot(a, wd_ref[sl, :], preferred_element_type=_F32)
    return x + 0.5 * _rms(acc, gain_ref[1:2, :])


def _subtiles(n_rows):
    return [slice(r, r + FFN_SUBTILE) for r in range(0, n_rows, FFN_SUBTILE)]


def _ffn_kernel(x_ref, gain_ref, wg_ref, wu_ref, wd_ref, o_ref):
    for rows in _subtiles(x_ref.shape[0]):
        o_ref[rows, :] = _swiglu_residual(x_ref[rows, :], gain_ref, wg_ref, wu_ref, wd_ref)


def _ffn(x, gains, wg, wu, wd):
    n = x.shape[0]
    tm = FFN_SUBTILES * FFN_SUBTILE
    row = lambda i: (i, 0)
    return pl.pallas_call(
        _ffn_kernel,
        out_shape=jax.ShapeDtypeStruct(x.shape, x.dtype),
        grid=(n // tm,),
        in_specs=[pl.BlockSpec((tm, D_MODEL), row), _resident((2, D_MODEL)),
                  _resident((D_MODEL, D_FF)), _resident((D_MODEL, D_FF)), _resident((D_FF, D_MODEL))],
        out_specs=pl.BlockSpec((tm, D_MODEL), row),
        compiler_params=_params(1),
        name="ffn",
    )(x, gains, wg, wu, wd)


_IN_ATT = 3 * ATT_W
_IN_RET = RET_HEADS * 3 * LANES
_IN_GATE = RET_PAD_W
_IN_CONV = 3 * CONV_W
_IN_TOTAL = _IN_ATT + _IN_RET + _IN_GATE + _IN_CONV
IN_CHUNK = 256


def _inproj_kernel(x_ref, gain_ref, w_ref, att1_ref, att4_ref, att16_ref, ret_ref, gate_ref, b_ref, u_ref,
                   stage, stage4):
    for t, rows in enumerate(_subtiles(x_ref.shape[0])):
        _inproj_subtile(t, rows, x_ref, gain_ref, w_ref, att1_ref, att4_ref, att16_ref, ret_ref, gate_ref, b_ref,
                        u_ref, stage, stage4)


def _inproj_subtile(t, rows, x_ref, gain_ref, w_ref, att1_ref, att4_ref, att16_ref, ret_ref, gate_ref, b_ref,
                    u_ref, stage, stage4):
    tm = FFN_SUBTILE
    h = _rms(x_ref[rows, :], gain_ref[...]).astype(_BF)
    c0 = 0
    def project(start, width):
        for c in range(0, width, IN_CHUNK):
            cw = min(IN_CHUNK, width - c)
            yield c, jnp.dot(h, w_ref[:, start + c:start + c + cw], preferred_element_type=_F32)

    n_slabs = _IN_ATT // LANES
    for c, z in project(c0, _IN_ATT):
        for s in range(c // LANES, (c + z.shape[1]) // LANES):
            zs = z[:, s * LANES - c:(s + 1) * LANES - c]
            if (s + 1) * LANES <= ATT_W:
                zs = zs * ATT_Q_SCALE
            stage[s] = zs
            att1_ref[0, s, rows, :] = zs.astype(_BF)
    d4, d16 = DILATIONS[1], DILATIONS[2]
    rows4, rows16 = tm // d4, tm // d16
    for s in range(n_slabs):
        for r4 in range(d4):
            x = stage[s, pl.ds(r4, rows4, stride=d4), :]
            att4_ref[0, s, r4, t * rows4:(t + 1) * rows4, :] = x.astype(_BF)
            stage4[s, r4 * rows4:(r4 + 1) * rows4, :] = x
        for r4 in range(d4):
            for j in range(d16 // d4):
                x = stage4[s, pl.ds(r4 * rows4 + j, rows16, stride=d16 // d4), :]
                att16_ref[0, s, r4 + d4 * j, t * rows16:(t + 1) * rows16, :] = x.astype(_BF)
    c0 += _IN_ATT

    def slabs(ref, c, z, dtype):
        for s in range(z.shape[1] // LANES):
            ref[0, c // LANES + s, rows, :] = z[:, s * LANES:(s + 1) * LANES].astype(dtype)

    for c, z in project(c0, _IN_RET):
        slabs(ret_ref, c, z, _BF)
    c0 += _IN_RET
    for c, z in project(c0, _IN_GATE):
        slabs(gate_ref, c, z, _F32)
    c0 += _IN_GATE
    (_, zb), (_, zc), (_, zu) = project(c0, _IN_CONV)
    b_ref[rows, :] = zb
    u_ref[rows, :] = zc * zu


def _inproj(x, gain, w, batch, seq):
    n = x.shape[0]
    tm = FFN_SUBTILES * FFN_SUBTILE
    tiles_per_seq = seq // tm
    row = lambda i: (i, 0)
    where = lambda i: (i // tiles_per_seq, i % tiles_per_seq)

    def slab_out(width, dtype, d=1):
        n_slabs = width // LANES
        if d == 1:
            return (jax.ShapeDtypeStruct((batch, n_slabs, seq, LANES), dtype),
                    pl.BlockSpec((1, n_slabs, tm, LANES), lambda i: (where(i)[0], 0, where(i)[1], 0)))
        return (jax.ShapeDtypeStruct((batch, n_slabs, d, seq // d, LANES), dtype),
                pl.BlockSpec((1, n_slabs, d, tm // d, LANES), lambda i: (where(i)[0], 0, 0, where(i)[1], 0)))

    flat = [(jax.ShapeDtypeStruct((n, CONV_W), _F32), pl.BlockSpec((tm, CONV_W), row))] * 2
    outs = [slab_out(_IN_ATT, _BF, d) for d in DILATIONS] + [slab_out(_IN_RET, _BF), slab_out(_IN_GATE, _F32)] + flat
    att1, att4, att16, ret, gate, b, u = pl.pallas_call(
        _inproj_kernel,
        out_shape=[o[0] for o in outs],
        grid=(n // tm,),
        in_specs=[pl.BlockSpec((tm, D_MODEL), row), _resident((1, D_MODEL)),
                  _resident((D_MODEL, _IN_TOTAL))],
        out_specs=[o[1] for o in outs],
        scratch_shapes=[pltpu.VMEM((_IN_ATT // LANES, FFN_SUBTILE, LANES), _F32)] * 2,
        compiler_params=_params(1),
        name="inproj",
    )(x, gain, w)
    att = [a.reshape(batch, _IN_ATT // LANES, seq, LANES) for a in (att1, att4, att16)]
    return att, ret, gate, b, u


ATT_GROUPS = 4
MERGE_UNROLL = 8
ATT_SLOTS = 2
ATT_VARIANTS = 6


def _att_kernel(*refs, seq):
    n_win = len(DILATIONS)
    group = seq // ATT_TQ // ATT_GROUPS
    qkv_refs = [refs[3 * w:3 * w + 3] for w in range(n_win)]
    o_ref, od, ld, bias_ref, s_buf, p_buf, m_buf = refs[3 * n_win:]
    pair = pl.program_id(0)
    lane = lax.broadcasted_iota(jnp.int32, (1, LANES), 1)
    head_a = lane < HEAD_DIM
    zero_bf = jnp.zeros((), _BF)

    mask_a = jnp.broadcast_to(jnp.where(head_a, 1.0, 0.0), (ATT_TK, LANES)).astype(_BF)
    mask_b = jnp.broadcast_to(jnp.where(head_a, 0.0, 1.0), (ATT_TK, LANES)).astype(_BF)

    def heads(x):
        return jnp.where(head_a, x, zero_bf), jnp.where(head_a, zero_bf, x)

    def slope(j):
        s = jnp.float32(_SLOPES[j])
        for p in range(1, ATT_HEADS // 2):
            s = jnp.where(pair == p, jnp.float32(_SLOPES[2 * p + j]), s)
        return s

    @pl.when(pl.program_id(1) == 0)
    def _():
        qi = lax.broadcasted_iota(jnp.int32, (ATT_TQ, ATT_TK), 0)
        kc = lax.broadcasted_iota(jnp.int32, (ATT_TQ, ATT_TK), 1)
        for w, d in enumerate(DILATIONS):
            single = seq // d == ATT_TQ
            variants = {3: (ATT_HALF, True, True)} if single else {
                0: (ATT_HALF, False, False), 1: (ATT_HALF, True, False), 2: (ATT_HALF, False, True)}
            variants[4] = (0, True, single)
            variants[5] = (2 * ATT_HALF, single, True)
            for variant, (shift, first, last) in variants.items():
                rel = jnp.abs(qi + shift - kc)
                valid = rel <= ATT_HALF
                if first:
                    valid = valid & (kc >= shift)
                if last:
                    valid = valid & (kc < shift + ATT_TQ)
                dist = (d * rel).astype(_F32)
                for j in range(2):
                    bias_ref[w, j, variant] = jnp.where(valid, (-slope(j) * dist) * LOG2E, NEG_INF)

    for w, d in enumerate(DILATIONS):
        q_ref, k_ref, v_ref = qkv_refs[w]
        sub_len = seq // d
        n_blocks = sub_len // ATT_TQ
        n_total = d * n_blocks

        def coords(blk):
            r = blk // n_blocks
            b = blk - r * n_blocks
            if isinstance(blk, int):
                at_start, at_end = int(blk == 0), int(blk == n_total - 1)
                variant = 4 if at_start else 5 if at_end else int(b == 0) + 2 * int(b == n_blocks - 1)
            else:
                at_start, at_end = jnp.where(blk == 0, 1, 0), jnp.where(blk == n_total - 1, 1, 0)
                edge = jnp.where(b == 0, 1, 0) + jnp.where(b == n_blocks - 1, 2, 0)
                variant = jnp.where(blk == 0, 4, jnp.where(blk == n_total - 1, 5, edge))
            q0 = _aligned(blk * ATT_TQ, ATT_TQ)
            t0 = _aligned(blk * ATT_TQ - ATT_HALF + ATT_HALF * (at_start - at_end), ATT_HALF)
            return r, b, q0, t0, variant

        def score(grp, slot):
            for u in range(group):
                _, _, q0, t0, variant = coords(grp * group + u)
                q2 = jnp.concatenate(heads(q_ref[0, 0,pl.ds(q0, ATT_TQ), :]), axis=0)
                s = lax.dot_general(q2, k_ref[0, 0,pl.ds(t0, ATT_TK), :], _NT, preferred_element_type=_F32)
                for j in range(2):
                    s_buf[slot, u, j] = s[j * ATT_TQ:(j + 1) * ATT_TQ] + bias_ref[w, j, variant]

        def softmax(slot):
            for u in range(group):
                tops = [jnp.max(s_buf[slot, u, j], axis=-1, keepdims=True) for j in range(2)]
                m_buf[slot, u] = jnp.where(head_a, tops[0], tops[1])
                for j in range(2):
                    p_buf[slot, u, j] = jnp.exp2(s_buf[slot, u, j] - tops[j]).astype(_BF)

        def output(grp, slot):
            for u in range(group):
                r, b, q0, t0, _ = coords(grp * group + u)
                v_a, v_b = heads(v_ref[0, 0,pl.ds(t0, ATT_TK), :])
                acc = jnp.dot(p_buf[slot, u, 0], jnp.concatenate([v_a, mask_a], axis=1),
                              preferred_element_type=_F32)
                acc = acc + jnp.dot(p_buf[slot, u, 1], jnp.concatenate([v_b, mask_b], axis=1),
                                    preferred_element_type=_F32)
                den = acc[:, LANES:]
                dst = pl.ds(r + d * b * ATT_TQ, ATT_TQ, stride=d) if d > 1 else pl.ds(q0, ATT_TQ)
                od[w, dst, :] = acc[:, :LANES] / den
                ld[w, dst, :] = m_buf[slot, u] + jnp.log2(den)

        n_groups = ATT_GROUPS

        score(0, 0)
        softmax(0)
        score(1, 1)

        def steady(t, carry):
            for slot in range(2):
                g = 2 * t + slot
                output(g, slot)
                softmax(1 - slot)
                score(g + 2, slot)
            return carry

        lax.fori_loop(0, n_groups // 2 - 1, steady, 0)
        output(n_groups - 2, 0)
        softmax(1)
        output(n_groups - 1, 1)

    def merge(it, carry):
        for u in range(MERGE_UNROLL):
            rows = pl.ds(pl.multiple_of((it * MERGE_UNROLL + u) * ATT_TQ, ATT_TQ), ATT_TQ)
            lses = [ld[w, rows, :] for w in range(len(DILATIONS))]
            top = jnp.maximum(jnp.maximum(lses[0], lses[1]), lses[2])
            wts = [jnp.exp2(l - top) for l in lses]
            num = wts[0] * od[0, rows, :] + wts[1] * od[1, rows, :] + wts[2] * od[2, rows, :]
            o_ref[0, 0, rows, :] = (num / (wts[0] + wts[1] + wts[2])).astype(o_ref.dtype)
        return carry

    lax.fori_loop(0, seq // ATT_TQ // MERGE_UNROLL, merge, 0)


def _attention(qkv_by_dilation, batch, seq):
    n_pairs = ATT_W // LANES
    group = seq // ATT_TQ // ATT_GROUPS
    blk = (1, 1, seq, LANES)
    part = lambda j: pl.BlockSpec(blk, lambda p, b: (b, j * n_pairs + p, 0, 0))
    return pl.pallas_call(
        functools.partial(_att_kernel, seq=seq),
        out_shape=jax.ShapeDtypeStruct((batch, n_pairs, seq, LANES), _BF),
        grid=(n_pairs, batch),
        in_specs=[part(j) for _ in DILATIONS for j in range(3)],
        out_specs=part(0),
        scratch_shapes=[pltpu.VMEM((len(DILATIONS), seq, LANES), _F32)] * 2
        + [pltpu.VMEM((len(DILATIONS), 2, ATT_VARIANTS, ATT_TQ, ATT_TK), _F32),
           pltpu.VMEM((ATT_SLOTS, group, 2, ATT_TQ, ATT_TK), _F32),
           pltpu.VMEM((ATT_SLOTS, group, 2, ATT_TQ, ATT_TK), _BF),
           pltpu.VMEM((ATT_SLOTS, group, ATT_TQ, LANES), _F32)],
        compiler_params=_params(2),
        name="attention",
    )(*[a for a in qkv_by_dilation for _ in range(3)])


RET_STEP_TOKENS = 4096
RET_GROUPS = 2


def _ret_kernel(logit_ref, qkv_ref, gate_ref, out_ref, *scratch, seq, heads):
    per_head = len(scratch) // heads
    for j in range(heads):
        _ret_head(pl.program_id(1) * heads + j, logit_ref,
                  *(qkv_ref.at[:, 3 * j + i:3 * j + i + 1] for i in range(3)),
                  gate_ref.at[:, j:j + 1], out_ref.at[:, j:j + 1],
                  *scratch[j * per_head:(j + 1) * per_head], seq=seq)


def _ret_head(head, logit_ref, q_ref, k_ref, v_ref, g_ref, o_ref, sf, sb, lhs_buf, mix_buf, *, seq):
    c = RET_CHUNK
    n_chunks = seq // c
    group = n_chunks // RET_GROUPS
    scale = RET_HEAD_DIM ** -0.5

    def log_sigmoid(x):
        return jnp.minimum(x, 0.0) - jnp.log(1.0 + jnp.exp(-jnp.abs(x)))

    lg_f = log_sigmoid(jnp.full((c, LANES), logit_ref[0, head], _F32))
    lg_b = log_sigmoid(jnp.full((c, LANES), logit_ref[1, head], _F32))
    pos = lax.broadcasted_iota(jnp.int32, (c, LANES), 0).astype(_F32)
    col = lax.broadcasted_iota(jnp.int32, (c, LANES), 1).astype(_F32)
    kw_f = jnp.exp(lg_f * (c - 1.0 - pos)) * scale
    kw_b = jnp.exp(lg_b * pos) * scale
    qw_f = jnp.exp(lg_f * (pos + 1.0))
    qw_b = jnp.exp(lg_b * (c - pos))
    g_f = jnp.exp(lg_f * c)
    g_b = jnp.exp(lg_b * c)
    rel = pos - col
    decay = jnp.where(rel >= 0, jnp.exp(lg_f * jnp.maximum(rel, 0.0)),
                      jnp.exp(lg_b * jnp.maximum(-rel, 0.0))) * scale
    lane = lax.broadcasted_iota(jnp.int32, (1, LANES), 1)
    real = lane < RET_HEAD_DIM

    for i in range(n_chunks):
        k = k_ref[0, 0, i * c:(i + 1) * c, :].astype(_F32)
        v = v_ref[0, 0, i * c:(i + 1) * c, :]
        sf[i] = lax.dot_general((k * kw_f).astype(_BF), v, _TN, preferred_element_type=_F32)
        sb[i] = lax.dot_general((k * kw_b).astype(_BF), v, _TN, preferred_element_type=_F32)

    def scan(j, states):
        fwd, bwd = states
        i = n_chunks - 1 - j
        inc_f, inc_b = sf[j], sb[i]
        sf[j] = fwd
        sb[i] = bwd
        return fwd * g_f + inc_f, bwd * g_b + inc_b

    zero_state = jnp.zeros((LANES, LANES), _F32)
    lax.fori_loop(0, n_chunks, scan, (zero_state, zero_state))

    def chunk_rows(grp, u):
        i = grp * group + u
        return i, pl.ds(_aligned(i * c, c), c)

    def weigh(grp, slot):
        for u in range(group):
            _, rows = chunk_rows(grp, u)
            q = q_ref[0, 0,rows, :]
            qf = q.astype(_F32)
            scores = lax.dot_general(q, k_ref[0, 0,rows, :], _NT, preferred_element_type=_F32) * decay
            lhs_buf[slot, u, 0] = scores.astype(_BF)
            lhs_buf[slot, u, 1] = (qf * qw_f).astype(_BF)
            lhs_buf[slot, u, 2] = (qf * qw_b).astype(_BF)

    def mix(grp, slot):
        for u in range(group):
            i, rows = chunk_rows(grp, u)
            o = jnp.dot(lhs_buf[slot, u, 0], v_ref[0, 0,rows, :], preferred_element_type=_F32)
            o = o + jnp.dot(lhs_buf[slot, u, 1], sf[i].astype(_BF), preferred_element_type=_F32)
            o = o + jnp.dot(lhs_buf[slot, u, 2], sb[i].astype(_BF), preferred_element_type=_F32)
            mix_buf[slot, u] = o

    def finish(grp, slot):
        for u in range(group):
            _, rows = chunk_rows(grp, u)
            o = mix_buf[slot, u]
            mu = jnp.sum(o, axis=-1, keepdims=True) * (1.0 / RET_HEAD_DIM)
            dev = jnp.where(real, o - mu, 0.0)
            var = jnp.sum(dev * dev, axis=-1, keepdims=True) * (1.0 / RET_HEAD_DIM)
            g = g_ref[0, 0, rows, :]
            o_ref[0, 0, rows, :] = (dev * lax.rsqrt(var + NORM_EPS) * (g / (1.0 + jnp.exp(-g)))).astype(o_ref.dtype)

    n_groups = RET_GROUPS
    weigh(0, 0)
    mix(0, 0)
    weigh(1, 1)

    def steady(t, carry):
        for slot in range(2):
            grp = 2 * t + slot
            finish(grp, slot)
            mix(grp + 1, 1 - slot)
            weigh(grp + 2, slot)
        return carry

    lax.fori_loop(0, n_groups // 2 - 1, steady, 0)
    finish(n_groups - 2, 0)
    mix(n_groups - 1, 1)
    finish(n_groups - 1, 1)


def _retention(qkv, gate, logits, batch, seq):
    hp = min(RET_HEADS, max(1, RET_STEP_TOKENS // seq))
    n_chunks = seq // RET_CHUNK
    group = n_chunks // RET_GROUPS
    slabs = lambda n: pl.BlockSpec((1, n, seq, LANES), lambda b, p: (b, p, 0, 0))
    per_head_scratch = [pltpu.VMEM((n_chunks, LANES, LANES), _F32)] * 2 + [
        pltpu.VMEM((2, group, 3, RET_CHUNK, LANES), _BF), pltpu.VMEM((2, group, RET_CHUNK, LANES), _F32)]
    return pl.pallas_call(
        functools.partial(_ret_kernel, seq=seq, heads=hp),
        out_shape=jax.ShapeDtypeStruct((batch, RET_HEADS, seq, LANES), _BF),
        grid=(batch, RET_HEADS // hp),
        in_specs=[pl.BlockSpec(memory_space=pltpu.SMEM), slabs(3 * hp), slabs(hp)],
        out_specs=slabs(hp),
        scratch_shapes=per_head_scratch * hp,
        compiler_params=_params(2),
        name="retention",
    )(logits, qkv, gate)


def _mix_out_ffn_kernel(att_ref, ret_ref, b_ref, u_ref, up_ref, un_ref, cw_ref, wa_ref, wr_ref, wc_ref,
                        mix_gain_ref, x_ref, ffn_gain_ref, wg_ref, wu_ref, wd_ref, o_ref, *, tiles_per_seq):
    tm = u_ref.shape[0]
    t = pl.program_id(0) % tiles_per_seq
    u = u_ref[...]
    prev_row = jnp.where(t == 0, 0.0, up_ref[7:8, :])
    next_row = jnp.where(t == tiles_per_seq - 1, 0.0, un_ref[0:1, :])
    row_id = lax.broadcasted_iota(jnp.int32, u.shape, 0)
    u_prev = jnp.where(row_id == 0, prev_row, pltpu.roll(u, 1, 0))
    u_next = jnp.where(row_id == tm - 1, next_row, pltpu.roll(u, tm - 1, 0))
    conv = cw_ref[0:1, :] * u_prev + cw_ref[1:2, :] * u + cw_ref[2:3, :] * u_next
    cv = (b_ref[...] * conv).astype(_BF)
    def wide(ref, rows):
        return jnp.concatenate([ref[0, s, rows, :] for s in range(ref.shape[1])], axis=1)

    for rows in _subtiles(tm):
        y = jnp.dot(wide(att_ref, rows), wa_ref[...], preferred_element_type=_F32)
        y = y + jnp.dot(wide(ret_ref, rows), wr_ref[...], preferred_element_type=_F32)
        y = y + jnp.dot(cv[rows], wc_ref[...], preferred_element_type=_F32)
        x_mid = x_ref[rows, :] + _rms(y, mix_gain_ref[...])
        o_ref[rows, :] = _swiglu_residual(x_mid, ffn_gain_ref, wg_ref, wu_ref, wd_ref)


def _mix_out_ffn(att, ret, b, u, conv_w, wa, wr, wc, mix_gain, x, ffn_gains, wg, wu, wd, seq):
    n = x.shape[0]
    tm = FFN_SUBTILES * FFN_SUBTILE
    sub = 8
    row = lambda i: (i, 0)
    prev_blk = lambda i: (jnp.maximum(i * (tm // sub) - 1, 0), 0)
    next_blk = lambda i: (jnp.minimum((i + 1) * (tm // sub), n // sub - 1), 0)
    tiles_per_seq = seq // tm
    slab_rows = lambda i: (i // tiles_per_seq, 0, i % tiles_per_seq, 0)
    return pl.pallas_call(
        functools.partial(_mix_out_ffn_kernel, tiles_per_seq=tiles_per_seq),
        out_shape=jax.ShapeDtypeStruct(x.shape, x.dtype),
        grid=(n // tm,),
        in_specs=[pl.BlockSpec((1, ATT_W // LANES, tm, LANES), slab_rows),
                  pl.BlockSpec((1, RET_HEADS, tm, LANES), slab_rows),
                  pl.BlockSpec((tm, CONV_W), row), pl.BlockSpec((tm, CONV_W), row),
                  pl.BlockSpec((sub, CONV_W), prev_blk), pl.BlockSpec((sub, CONV_W), next_blk),
                  _resident((3, CONV_W)), _resident((ATT_W, D_MODEL)), _resident((RET_PAD_W, D_MODEL)),
                  _resident((CONV_W, D_MODEL)), _resident((1, D_MODEL)),
                  pl.BlockSpec((tm, D_MODEL), row), _resident((2, D_MODEL)),
                  _resident((D_MODEL, D_FF)), _resident((D_MODEL, D_FF)), _resident((D_FF, D_MODEL))],
        out_specs=pl.BlockSpec((tm, D_MODEL), row),
        compiler_params=_params(1),
        name="mix_out_ffn",
    )(att, ret, b, u, u, u, conv_w, wa, wr, wc, mix_gain, x, ffn_gains, wg, wu, wd)


def _pad_heads(w, axis):
    shape = w.shape[:axis] + (RET_HEADS, RET_HEAD_DIM) + w.shape[axis + 1:]
    pad = [(0, 0)] * (len(shape))
    pad[axis + 1] = (0, LANES - RET_HEAD_DIM)
    w = jnp.pad(w.reshape(shape), pad)
    return w.reshape(w.shape[:axis] + (RET_PAD_W,) + w.shape[axis + 2:])


def _mix_in_weights(w):
    att = w[:, :_IN_ATT]
    c0 = _IN_ATT
    parts = [_pad_heads(w[:, c0 + j * RET_W:c0 + (j + 1) * RET_W], 1).reshape(D_MODEL, RET_HEADS, LANES)
             for j in range(4)]
    qkv = jnp.stack(parts[:3], axis=2).reshape(D_MODEL, _IN_RET)
    gate = parts[3].reshape(D_MODEL, _IN_GATE)
    conv = w[:, c0 + 4 * RET_W:]
    return jnp.concatenate([att, qkv, gate, conv], axis=1).astype(_BF)


def _trunk(x, batch, seq, weights):
    for layer in weights:
        x = _ffn(x, layer["gain"][0:2], *layer["ffn1"])
        att_qkv, ret_qkv, gate, b, u = _inproj(x, layer["gain"][2:3], layer["w_in"], batch, seq)
        att = _attention(att_qkv, batch, seq)
        ret = _retention(ret_qkv, gate, layer["logit"], batch, seq)
        x = _mix_out_ffn(att, ret, b, u, layer["conv_w"], *layer["w_out"], layer["gain"][3:4], x,
                         layer["gain"][4:6], *layer["ffn2"], seq)
    return x


def _cast_kernel(w_ref, o_ref):
    o_ref[...] = w_ref[0].astype(o_ref.dtype)


def _layer_bf16(w, layer):
    _, rows, cols = w.shape
    tr = CAST_ROWS
    return pl.pallas_call(
        _cast_kernel,
        out_shape=jax.ShapeDtypeStruct((rows, cols), _BF),
        grid=(rows // tr,),
        in_specs=[pl.BlockSpec((1, tr, cols), lambda i: (layer, i, 0))],
        out_specs=pl.BlockSpec((tr, cols), lambda i: (i, 0)),
        compiler_params=_params(1),
        name="cast_bf16",
    )(w)


def kernel(x_prompt, x_sample, norm_gain, ffn1_w_gate, ffn1_w_up, ffn1_w_down, w_mix_in, conv_w,
           ret_decay_logit, w_mix_out, ffn2_w_gate, ffn2_w_up, ffn2_w_down):
    weights = []
    for l in range(DEPTH):
        wo = w_mix_out[l]
        weights.append(dict(
            gain=norm_gain[l],
            ffn1=tuple(_layer_bf16(w, l) for w in (ffn1_w_gate, ffn1_w_up, ffn1_w_down)),
            ffn2=tuple(_layer_bf16(w, l) for w in (ffn2_w_gate, ffn2_w_up, ffn2_w_down)),
            w_in=_mix_in_weights(w_mix_in[l]),
            conv_w=conv_w[l],
            logit=ret_decay_logit[l],
            w_out=(wo[:ATT_W].astype(_BF), _pad_heads(wo[ATT_W:ATT_W + RET_W], 0).astype(_BF),
                   wo[ATT_W + RET_W:].astype(_BF)),
        ))
    outs = []
    for x in (x_prompt, x_sample):
        batch, seq, _ = x.shape
        outs.append(_trunk(x.reshape(batch * seq, D_MODEL), batch, seq, weights).reshape(x.shape))
    return tuple(outs)
```
